```python
import math
import jax, jax.numpy as jnp
from jax import lax
import numpy as np

D_MODEL = 1024
BATCH = 16
SEQ = 2048
DEPTH = 1

D_MIX = D_MODEL
ATTN_WIDTH = D_MIX // 2
HGRN_WIDTH = D_MIX - ATTN_WIDTH
ATTN_HEAD_DIM = 64
N_Q_HEADS = ATTN_WIDTH // ATTN_HEAD_DIM
N_KV_HEADS = 2
KV_WIDTH = N_KV_HEADS * ATTN_HEAD_DIM
WINDOW = 128
ATTN_BLOCK = 128
HGRN_HEAD_DIM = 128
N_HGRN_HEADS = HGRN_WIDTH // HGRN_HEAD_DIM
HGRN_CHUNK = 64
N_EXPERTS = 32
TOP_K = 4
D_EXPERT = D_MODEL
SWIGLU_LIMIT = 7.0
SWIGLU_ALPHA = 1.702
LN_EPS = 1e-5
RMS_EPS = 1e-6
DEEPNORM_ALPHA = (2.0 * DEPTH) ** 0.25
DEEPNORM_BETA = (8.0 * DEPTH) ** -0.25
IN_WIDTHS = (ATTN_WIDTH, KV_WIDTH, KV_WIDTH, HGRN_WIDTH, HGRN_WIDTH, HGRN_WIDTH, HGRN_WIDTH, HGRN_WIDTH)
IN_COLS = sum(IN_WIDTHS)
IN_SPLITS = tuple(int(c) for c in np.cumsum(IN_WIDTHS)[:-1])

kernel_name = 'hymba_swa_hgrn2_moe_encoder'


def layer_norm(x, g, b):
    xf = x.astype(jnp.float32)
    mu = jnp.mean(xf, axis=-1, keepdims=True)
    var = jnp.mean(jnp.square(xf - mu), axis=-1, keepdims=True)
    y = (xf - mu) * lax.rsqrt(var + LN_EPS) * g.astype(jnp.float32) + b.astype(jnp.float32)
    return y.astype(x.dtype)


def alibi_slopes(n):
    return jnp.asarray([2.0 ** (-8.0 * (h + 1) / n) for h in range(n)], jnp.float32)


def windowed_gqa(q, k, v, sink):
    b, s = q.shape[0], q.shape[1]
    nb = s // ATTN_BLOCK
    g, r = N_KV_HEADS, N_Q_HEADS // N_KV_HEADS
    qb = q.reshape(b, nb, ATTN_BLOCK, g, r, ATTN_HEAD_DIM)
    pad = ((0, 0), (ATTN_BLOCK, ATTN_BLOCK), (0, 0), (0, 0))

    def band(t):
        tp = jnp.pad(t, pad).reshape(b, nb + 2, ATTN_BLOCK, g, ATTN_HEAD_DIM)
        return jnp.concatenate([tp[:, :-2], tp[:, 1:-1], tp[:, 2:]], axis=2)

    kb, vb = band(k), band(v)
    scores = jnp.einsum('bnqgrd,bnkgd->bgrnqk', qb, kb).astype(jnp.float32) * (1.0 / math.sqrt(ATTN_HEAD_DIM))
    qpos = jnp.arange(nb)[:, None, None] * ATTN_BLOCK + jnp.arange(ATTN_BLOCK)[None, :, None]
    kpos = jnp.arange(nb)[:, None, None] * ATTN_BLOCK - ATTN_BLOCK + jnp.arange(3 * ATTN_BLOCK)[None, None, :]
    dist = jnp.abs(kpos - qpos)
    valid = (dist <= WINDOW) & (kpos >= 0) & (kpos < s)
    slopes = alibi_slopes(N_Q_HEADS).reshape(g, r)[:, :, None, None, None]
    scores = jnp.where(valid, scores - slopes * dist.astype(jnp.float32), -jnp.inf)
    sink_l = sink.astype(jnp.float32).reshape(g, r)[None, :, :, None, None, None]
    m = jnp.maximum(jnp.max(scores, axis=-1, keepdims=True), sink_l)
    e = jnp.exp(scores - m)
    p = e / (jnp.sum(e, axis=-1, keepdims=True) + jnp.exp(sink_l - m))
    out = jnp.einsum('bgrnqk,bnkgd->bnqgrd', p.astype(vb.dtype), vb)
    return out.reshape(b, s, N_Q_HEADS * ATTN_HEAD_DIM)


def gated_linear_scan(q, k, v, log_f):
    b, s, h, dk = q.shape
    dv = v.shape[-1]
    c = HGRN_CHUNK
    n = s // c
    rs = lambda t: t.reshape(b, n, c, h, t.shape[-1])
    q, k, v, log_f = rs(q), rs(k), rs(v), rs(log_f)
    cum = jnp.cumsum(log_f, axis=2)
    last = cum[:, :, -1]
    ref = cum[:, :, c // 2][:, :, None]
    q_rel = q * jnp.exp(cum - ref)
    k_rel = k * jnp.exp(ref - cum)
    lower = jnp.tril(jnp.ones((c, c), dtype=bool))
    a = jnp.einsum('bnthd,bnshd->bnhts', q_rel, k_rel)
    a = jnp.where(lower, a, 0.0)
    o_intra = jnp.einsum('bnhts,bnshv->bnthv', a, v)
    chunk_kv = jnp.einsum('bnshd,bnshv->bnhdv', k * jnp.exp(last[:, :, None] - cum), v)
    decay = jnp.exp(last)

    def step(state, inp):
        dec, kv = inp
        return dec[..., None] * state + kv, state

    init = jnp.zeros((b, h, dk, dv), jnp.float32)
    _, s_prev = lax.scan(step, init, (jnp.swapaxes(decay, 0, 1), jnp.swapaxes(chunk_kv, 0, 1)))
    s_prev = jnp.swapaxes(s_prev, 0, 1)
    o_inter = jnp.einsum('bnthd,bnhdv->bnthv', q * jnp.exp(cum), s_prev)
    return (o_intra + o_inter).reshape(b, s, h, dv)


def hgrn2_bidirectional(q, i, zf_fwd, zf_bwd, z_g, lb, norm_g):
    b, s = q.shape[0], q.shape[1]
    heads = lambda t: t.astype(jnp.float32).reshape(b, s, N_HGRN_HEADS, HGRN_HEAD_DIM)
    qh, vh = heads(q), heads(i)

    def gate(zf, lb_d):
        f = lb_d + (1.0 - lb_d) * jax.nn.sigmoid(zf.astype(jnp.float32))
        f = heads(f)
        return jnp.log(f), 1.0 - f

    log_ff, k_f = gate(zf_fwd, lb[0])
    log_fb, k_b = gate(zf_bwd, lb[1])
    flip = lambda t: jnp.flip(t, axis=1)
    o_f = gated_linear_scan(qh, k_f, vh, log_ff)
    o_b = flip(gated_linear_scan(flip(qh), flip(k_b), flip(vh), flip(log_fb)))
    o = o_f + o_b
    o = o * lax.rsqrt(jnp.mean(jnp.square(o), axis=-1, keepdims=True) + RMS_EPS)
    o = o.reshape(b, s, HGRN_WIDTH) * norm_g.astype(jnp.float32) * jax.nn.silu(z_g.astype(jnp.float32))
    return o.astype(q.dtype)


def hybrid_mixer(h, w_in, sink, lb, norm_g, w_out):
    b, s, _ = h.shape
    proj = h @ w_in
    q_a, k_a, v_a, q_h, i_h, zf_fwd, zf_bwd, z_g = jnp.split(proj, IN_SPLITS, axis=-1)
    attn = windowed_gqa(q_a.reshape(b, s, N_Q_HEADS, ATTN_HEAD_DIM),
                        k_a.reshape(b, s, N_KV_HEADS, ATTN_HEAD_DIM),
                        v_a.reshape(b, s, N_KV_HEADS, ATTN_HEAD_DIM), sink)
    rec = hgrn2_bidirectional(q_h, i_h, zf_fwd, zf_bwd, z_g, lb, norm_g)
    return jnp.concatenate([attn.astype(h.dtype), rec], axis=-1) @ w_out


def moe_ffn(x, w_router, b_router, w_up, b_up, w_down, b_down):
    logits = (x @ w_router + b_router).astype(jnp.float32)
    top_v, top_i = lax.top_k(logits, TOP_K)
    gates = jax.nn.softmax(top_v, axis=-1)
    gate_full = jnp.sum(jax.nn.one_hot(top_i, N_EXPERTS, dtype=jnp.float32) * gates[..., None], axis=1)
    gate_full = gate_full.astype(x.dtype)
    out = jnp.zeros_like(x)
    for e in range(N_EXPERTS):
        hid = x @ w_up[e] + b_up[e]
        x_glu = jnp.minimum(hid[..., ::2], SWIGLU_LIMIT)
        x_lin = jnp.clip(hid[..., 1::2], -SWIGLU_LIMIT, SWIGLU_LIMIT)
        act = x_glu * jax.nn.sigmoid(SWIGLU_ALPHA * x_glu) * (x_lin + 1.0)
        out = out + gate_full[:, e:e + 1] * (act @ w_down[e] + b_down[e])
    return out


def setup_inputs(seed: int = 0) -> dict:
    key = jax.random.key(seed)
    ks = jax.random.split(key, 20)
    f32 = jnp.float32
    nrm = lambda k, shape, sc: jax.random.normal(k, shape, f32) * sc
    col_scale = jnp.concatenate([
        jnp.full((w,), sc, f32) for w, sc in zip(
            IN_WIDTHS, (1.0, 1.0, DEEPNORM_BETA, 1.0, DEEPNORM_BETA, 1.0, 1.0, 1.0))])
    return {
        'x': nrm(ks[0], (BATCH, SEQ, D_MODEL), 1.0),
        'emb_ln_g': 1.0 + nrm(ks[1], (D_MODEL,), 0.02),
        'emb_ln_b': nrm(ks[2], (D_MODEL,), 0.02),
        'w_in': nrm(ks[3], (DEPTH, D_MODEL, IN_COLS), D_MODEL ** -0.5) * col_scale,
        'attn_sink': nrm(ks[4], (DEPTH, N_Q_HEADS), 0.5),
        'hgrn_lb_logits': nrm(ks[5], (2, DEPTH + 1, HGRN_WIDTH), 0.1),
        'hgrn_norm_g': 1.0 + nrm(ks[6], (DEPTH, HGRN_WIDTH), 0.02),
        'w_out': nrm(ks[7], (DEPTH, D_MIX, D_MODEL), D_MIX ** -0.5 * DEEPNORM_BETA),
        'ln1_g': 1.0 + nrm(ks[8], (DEPTH, D_MODEL), 0.02),
        'ln1_b': nrm(ks[9], (DEPTH, D_MODEL), 0.02),
        'w_router': nrm(ks[10], (DEPTH, D_MODEL, N_EXPERTS), D_MODEL ** -0.5),
        'b_router': nrm(ks[11], (DEPTH, N_EXPERTS), 0.01),
        'w_up': nrm(ks[12], (DEPTH, N_EXPERTS, D_MODEL, 2 * D_EXPERT), D_MODEL ** -0.5),
        'b_up': nrm(ks[13], (DEPTH, N_EXPERTS, 2 * D_EXPERT), 0.01),
        'w_down': nrm(ks[14], (DEPTH, N_EXPERTS, D_EXPERT, D_MODEL), D_EXPERT ** -0.5 * DEEPNORM_BETA),
        'b_down': nrm(ks[15], (DEPTH, N_EXPERTS, D_MODEL), 0.01),
        'ln2_g': 1.0 + nrm(ks[16], (DEPTH, D_MODEL), 0.02),
        'ln2_b': nrm(ks[17], (DEPTH, D_MODEL), 0.02),
    }


def reference(x, emb_ln_g, emb_ln_b, w_in, attn_sink, hgrn_lb_logits, hgrn_norm_g, w_out,
              ln1_g, ln1_b, w_router, b_router, w_up, b_up, w_down, b_down, ln2_g, ln2_b):
    h = layer_norm(x, emb_ln_g, emb_ln_b)
    lb_all = jnp.cumsum(jax.nn.softmax(hgrn_lb_logits.astype(jnp.float32), axis=1), axis=1)
    for l in range(DEPTH):
        mix = hybrid_mixer(h, w_in[l], attn_sink[l], lb_all[:, l], hgrn_norm_g[l], w_out[l])
        h = layer_norm(DEEPNORM_ALPHA * h + mix, ln1_g[l], ln1_b[l])
        ffn = moe_ffn(h.reshape(-1, D_MODEL), w_router[l], b_router[l], w_up[l], b_up[l],
                      w_down[l], b_down[l]).reshape(h.shape)
        h = layer_norm(DEEPNORM_ALPHA * h + ffn, ln2_g[l], ln2_b[l])
    return h
```

```python
import functools
import math

import jax
import jax.numpy as jnp
from jax import lax
from jax.experimental import pallas as pl
from jax.experimental.pallas import tpu as pltpu

F32 = jnp.float32
BF16 = jnp.bfloat16

ATTN_HEAD_DIM = 64
N_Q_HEADS = 8
N_KV_HEADS = 2
ATTN_WIDTH = N_Q_HEADS * ATTN_HEAD_DIM
KV_WIDTH = N_KV_HEADS * ATTN_HEAD_DIM
WINDOW = 128
ATTN_BLOCK = 128
HGRN_HEAD_DIM = 128
N_HGRN_HEADS = 4
HGRN_WIDTH = N_HGRN_HEADS * HGRN_HEAD_DIM
HGRN_CHUNK = 64
TOP_K = 4
SWIGLU_LIMIT = 7.0
SWIGLU_ALPHA = 1.702
LN_EPS = 1e-5
RMS_EPS = 1e-6
DEPTH = 1
DEEPNORM_ALPHA = (2.0 * DEPTH) ** 0.25

VMEM_LIMIT_BYTES = 48 * 1024 * 1024
TOKEN_TILE = 512
ROUTE_TILE = 256
FFN_TILE = 512


def _params(*sem):
    return pltpu.CompilerParams(dimension_semantics=sem, vmem_limit_bytes=VMEM_LIMIT_BYTES)


def _layer_norm(x, g, b):
    mu = jnp.mean(x, axis=-1, keepdims=True)
    xc = x - mu
    var = jnp.mean(xc * xc, axis=-1, keepdims=True)
    return xc * lax.rsqrt(var + LN_EPS) * g + b


def _split3(x):
    hi = x.astype(BF16)
    r = x - hi.astype(F32)
    mid = r.astype(BF16)
    lo = (r - mid.astype(F32)).astype(BF16)
    return hi, mid, lo


def _inproj_kernel(x_ref, g_ref, b_ref, w_ref, o_ref, *, col_chunk):
    h = _layer_norm(x_ref[...], g_ref[...], b_ref[...]).astype(BF16)
    n_cols = o_ref.shape[1]
    for c in range(0, n_cols, col_chunk):
        o_ref[:, c:c + col_chunk] = jnp.dot(
            h, w_ref[:, c:c + col_chunk], preferred_element_type=F32).astype(o_ref.dtype)


def _inproj(x2, g, b, w_bf16):
    n, d = x2.shape
    cols = w_bf16.shape[1]
    return pl.pallas_call(
        functools.partial(_inproj_kernel, col_chunk=256),
        grid=(n // TOKEN_TILE,),
        in_specs=[
            pl.BlockSpec((TOKEN_TILE, d), lambda i: (i, 0)),
            pl.BlockSpec((1, d), lambda i: (0, 0)),
            pl.BlockSpec((1, d), lambda i: (0, 0)),
            pl.BlockSpec((d, cols), lambda i: (0, 0)),
        ],
        out_specs=pl.BlockSpec((TOKEN_TILE, cols), lambda i: (i, 0)),
        out_shape=jax.ShapeDtypeStruct((n, cols), BF16),
        compiler_params=_params("parallel"),
    )(x2, g, b, w_bf16)


def _attn_kernel(sink_ref, q_ref, kp_ref, ko_ref, kn_ref, o_ref, *, seq):
    n = pl.program_id(1)
    blk = ATTN_BLOCK
    q = q_ref[0]
    kv = jnp.concatenate([kp_ref[0], ko_ref[0], kn_ref[0]], axis=0)
    qpos = lax.broadcasted_iota(jnp.int32, (blk, 3 * blk), 0)
    krel = lax.broadcasted_iota(jnp.int32, (blk, 3 * blk), 1) - blk
    dist = jnp.abs(krel - qpos)
    kpos = krel + n * blk
    valid = (dist <= WINDOW) & (kpos >= 0) & (kpos < seq)
    distf = dist.astype(F32)
    scale = 1.0 / math.sqrt(ATTN_HEAD_DIM)
    rep = N_Q_HEADS // N_KV_HEADS
    outs = []
    for h in range(N_Q_HEADS):
        g = h // rep
        slope = 2.0 ** (-8.0 * (h + 1) / N_Q_HEADS)
        qh = q[:, h * ATTN_HEAD_DIM:(h + 1) * ATTN_HEAD_DIM]
        kg = kv[:, g * ATTN_HEAD_DIM:(g + 1) * ATTN_HEAD_DIM]
        vg = kv[:, KV_WIDTH + g * ATTN_HEAD_DIM:KV_WIDTH + (g + 1) * ATTN_HEAD_DIM]
        s = lax.dot_general(qh, kg, (((1,), (1,)), ((), ())), preferred_element_type=F32) * scale
        s = jnp.where(valid, s - slope * distf, -jnp.inf)
        sink = sink_ref[h]
        m = jnp.maximum(jnp.max(s, axis=-1, keepdims=True), sink)
        e = jnp.exp(s - m)
        denom = jnp.sum(e, axis=-1, keepdims=True) + jnp.exp(sink - m)
        pv = jnp.dot(e.astype(BF16), vg, preferred_element_type=F32)
        outs.append(pv / denom)
    o_ref[0] = jnp.concatenate(outs, axis=1).astype(o_ref.dtype)


def _attention(proj3, sink):
    b, s, _ = proj3.shape
    nb = s // ATTN_BLOCK
    kv_col = ATTN_WIDTH // (2 * KV_WIDTH)
    kv_spec = lambda f: pl.BlockSpec((1, ATTN_BLOCK, 2 * KV_WIDTH), f)
    return pl.pallas_call(
        functools.partial(_attn_kernel, seq=s),
        grid=(b, nb),
        in_specs=[
            pl.BlockSpec(memory_space=pltpu.SMEM),
            pl.BlockSpec((1, ATTN_BLOCK, ATTN_WIDTH), lambda i, j: (i, j, 0)),
            kv_spec(lambda i, j: (i, jnp.maximum(j - 1, 0), kv_col)),
            kv_spec(lambda i, j: (i, j, kv_col)),
            kv_spec(lambda i, j: (i, jnp.minimum(j + 1, nb - 1), kv_col)),
        ],
        out_specs=pl.BlockSpec((1, ATTN_BLOCK, ATTN_WIDTH), lambda i, j: (i, j, 0)),
        out_shape=jax.ShapeDtypeStruct((b, s, ATTN_WIDTH), BF16),
        compiler_params=_params("parallel", "parallel"),
    )(sink, proj3, proj3, proj3, proj3)


def _hgrn_chunk(q, v, z, lb, state_t, tri, forward):
    c = q.shape[0]
    dk = q.shape[1]
    f = lb + (1.0 - lb) * jax.nn.sigmoid(z)
    log_f = jnp.log(f)
    k = 1.0 - f
    pieces = jnp.concatenate(_split3(log_f), axis=1)
    cum3 = jnp.dot(tri, pieces, preferred_element_type=F32)
    cum = cum3[:, :dk] + cum3[:, dk:2 * dk] + cum3[:, 2 * dk:]
    if forward:
        ref = cum[c // 2:c // 2 + 1, :]
        last = cum[c - 1:c, :]
    else:
        ref = cum[c - 1 - c // 2:c - c // 2, :]
        last = cum[0:1, :]
    q_rel = (q * jnp.exp(cum - ref)).astype(BF16)
    k_rel = (k * jnp.exp(ref - cum)).astype(BF16)
    a = lax.dot_general(q_rel, k_rel, (((1,), (1,)), ((), ())), preferred_element_type=F32)
    a = jnp.where(tri > 0, a, 0.0).astype(BF16)
    o_intra = jnp.dot(a, v, preferred_element_type=F32)
    k_dec = (k * jnp.exp(last - cum)).astype(BF16)
    kv_t = lax.dot_general(v, k_dec, (((0,), (0,)), ((), ())), preferred_element_type=F32)
    q_dec = (q * jnp.exp(cum)).astype(BF16)
    o_inter = lax.dot_general(q_dec, state_t.astype(BF16), (((1,), (1,)), ((), ())),
                              preferred_element_type=F32)
    new_state_t = state_t * jnp.exp(last) + kv_t
    return o_intra + o_inter, new_state_t


def _hgrn_kernel(q_ref, v_ref, zf_ref, zb_ref, zg_ref, lb_ref, ng_ref, o_ref, of_ref, ob_ref):
    c = HGRN_CHUNK
    seq = q_ref.shape[1]
    n_chunks = seq // c
    row = lax.broadcasted_iota(jnp.int32, (c, c), 0)
    col = lax.broadcasted_iota(jnp.int32, (c, c), 1)
    lower = jnp.where(row >= col, 1.0, 0.0).astype(BF16)
    upper = jnp.where(row <= col, 1.0, 0.0).astype(BF16)
    lb_f = lb_ref[0:1, :]
    lb_b = lb_ref[1:2, :]

    def body(i, carry):
        st_f, st_b = carry
        sl_f = pl.ds(pl.multiple_of(i * c, c), c)
        sl_b = pl.ds(pl.multiple_of((n_chunks - 1 - i) * c, c), c)
        o_f, st_f = _hgrn_chunk(q_ref[0, sl_f, :].astype(F32), v_ref[0, sl_f, :],
                                zf_ref[0, sl_f, :].astype(F32), lb_f, st_f, lower, True)
        o_b, st_b = _hgrn_chunk(q_ref[0, sl_b, :].astype(F32), v_ref[0, sl_b, :],
                                zb_ref[0, sl_b, :].astype(F32), lb_b, st_b, upper, False)
        of_ref[sl_f, :] = o_f
        ob_ref[sl_b, :] = o_b
        return st_f, st_b

    zero = jnp.zeros((v_ref.shape[2], q_ref.shape[2]), F32)
    lax.fori_loop(0, n_chunks, body, (zero, zero))

    o = of_ref[...] + ob_ref[...]
    o = o * lax.rsqrt(jnp.mean(o * o, axis=-1, keepdims=True) + RMS_EPS)
    zg = zg_ref[0].astype(F32)
    o_ref[0] = (o * ng_ref[...] * (zg * jax.nn.sigmoid(zg))).astype(o_ref.dtype)


def _hgrn(proj3, lb, norm_g):
    b, s, _ = proj3.shape
    hd = HGRN_HEAD_DIM
    base = (ATTN_WIDTH + 2 * KV_WIDTH) // hd
    col_spec = lambda part: pl.BlockSpec((1, s, hd), lambda i, h: (i, 0, base + part * N_HGRN_HEADS + h))
    return pl.pallas_call(
        _hgrn_kernel,
        grid=(b, N_HGRN_HEADS),
        in_specs=[
            col_spec(0), col_spec(1), col_spec(2), col_spec(3), col_spec(4),
            pl.BlockSpec((2, hd), lambda i, h: (0, h)),
            pl.BlockSpec((1, hd), lambda i, h: (0, h)),
        ],
        out_specs=pl.BlockSpec((1, s, hd), lambda i, h: (i, 0, h)),
        out_shape=jax.ShapeDtypeStruct((b, s, HGRN_WIDTH), BF16),
        scratch_shapes=[pltpu.VMEM((s, hd), F32), pltpu.VMEM((s, hd), F32)],
        compiler_params=_params("parallel", "parallel"),
    )(proj3, proj3, proj3, proj3, proj3, lb, norm_g)


def _outproj_kernel(x_ref, attn_ref, rec_ref, wo_ref, g0_ref, b0_ref, g1_ref, b1_ref,
                    wr_ref, br_ref, h1_ref, pk_ref, ti_ref, gt_ref):
    half = attn_ref.shape[1]
    h0 = _layer_norm(x_ref[...], g0_ref[...], b0_ref[...])
    mix = (jnp.dot(attn_ref[...], wo_ref[:half, :], preferred_element_type=F32)
           + jnp.dot(rec_ref[...], wo_ref[half:, :], preferred_element_type=F32))
    h1 = _layer_norm(DEEPNORM_ALPHA * h0 + mix, g1_ref[...], b1_ref[...])
    h1_ref[...] = h1

    d2 = h1.shape[1] // 2
    lo_bits = lax.bitcast_convert_type(h1[:, :d2].astype(BF16).astype(F32), jnp.uint32)
    hi_bits = lax.bitcast_convert_type(h1[:, d2:].astype(BF16).astype(F32), jnp.uint32)
    pk_ref[...] = hi_bits | (lo_bits >> 16)

    h_hi = h1.astype(BF16)
    h_lo = (h1 - h_hi.astype(F32)).astype(BF16)
    w_hi = wr_ref[0]
    w_lo = wr_ref[1]
    logits = (jnp.dot(h_hi, w_hi, preferred_element_type=F32)
              + jnp.dot(h_hi, w_lo, preferred_element_type=F32)
              + jnp.dot(h_lo, w_hi, preferred_element_type=F32)) + br_ref[...]

    n_exp = logits.shape[1]
    lane = lax.broadcasted_iota(jnp.int32, logits.shape, 1)
    vals, idxs = [], []
    cur = logits
    for _ in range(TOP_K):
        m = jnp.max(cur, axis=-1, keepdims=True)
        idx = jnp.min(jnp.where(cur == m, lane, n_exp), axis=-1, keepdims=True)
        vals.append(m)
        idxs.append(idx)
        cur = jnp.where(lane == idx, -jnp.inf, cur)
    top_v = jnp.concatenate(vals, axis=1)
    e = jnp.exp(top_v - vals[0])
    gt_ref[...] = e / jnp.sum(e, axis=-1, keepdims=True)
    ti_ref[...] = jnp.concatenate(idxs, axis=1)


def _outproj(x2, attn2, rec2, wo_bf16, g0, b0, g1, b1, wr2, br):
    n, d = x2.shape
    half = attn2.shape[1]
    n_exp = br.shape[1]
    tile = lambda w: pl.BlockSpec((TOKEN_TILE, w), lambda i: (i, 0))
    vec = pl.BlockSpec((1, d), lambda i: (0, 0))
    return pl.pallas_call(
        _outproj_kernel,
        grid=(n // TOKEN_TILE,),
        in_specs=[
            tile(d), tile(half), tile(half),
            pl.BlockSpec((d, d), lambda i: (0, 0)),
            vec, vec, vec, vec,
            pl.BlockSpec((2, d, n_exp), lambda i: (0, 0, 0)),
            pl.BlockSpec((1, n_exp), lambda i: (0, 0)),
        ],
        out_specs=[tile(d), tile(d // 2), tile(TOP_K), tile(TOP_K)],
        out_shape=[
            jax.ShapeDtypeStruct((n, d), F32),
            jax.ShapeDtypeStruct((n, d // 2), jnp.uint32),
            jax.ShapeDtypeStruct((n, TOP_K), jnp.int32),
            jax.ShapeDtypeStruct((n, TOP_K), F32),
        ],
        compiler_params=_params("parallel"),
    )(x2, attn2, rec2, wo_bf16, g0, b0, g1, b1, wr2, br)


def _rank_kernel(ti_ref, rank_ref, cnt_ref, run_ref, *, n_exp):
    i = pl.program_id(0)

    @pl.when(i == 0)
    def _():
        run_ref[...] = jnp.zeros_like(run_ref)

    ti = ti_ref[...]
    t = ti.shape[0]
    lane = lax.broadcasted_iota(jnp.int32, (t, n_exp), 1)
    hots = [jnp.where(lane == ti[:, k:k + 1], 1.0, 0.0) for k in range(TOP_K)]
    member = hots[0]
    for k in range(1, TOP_K):
        member = member + hots[k]
    row = lax.broadcasted_iota(jnp.int32, (t, t), 0)
    col = lax.broadcasted_iota(jnp.int32, (t, t), 1)
    strict_lower = jnp.where(row > col, 1.0, 0.0).astype(BF16)
    before = jnp.dot(strict_lower, member.astype(BF16), preferred_element_type=F32) + run_ref[...]
    ranks = [jnp.sum(hots[k] * before, axis=-1, keepdims=True) for k in range(TOP_K)]
    rank_ref[...] = jnp.concatenate(ranks, axis=1).astype(jnp.int32)
    run_ref[...] = run_ref[...] + jnp.sum(member, axis=0, keepdims=True)
    cnt_ref[...] = run_ref[...].astype(jnp.int32)


def _ranks(top_i, n_exp):
    n = top_i.shape[0]
    return pl.pallas_call(
        functools.partial(_rank_kernel, n_exp=n_exp),
        grid=(n // TOKEN_TILE,),
        in_specs=[pl.BlockSpec((TOKEN_TILE, TOP_K), lambda i: (i, 0))],
        out_specs=[pl.BlockSpec((TOKEN_TILE, TOP_K), lambda i: (i, 0)),
                   pl.BlockSpec((1, n_exp), lambda i: (0, 0))],
        out_shape=[jax.ShapeDtypeStruct((n, TOP_K), jnp.int32),
                   jax.ShapeDtypeStruct((1, n_exp), jnp.int32)],
        scratch_shapes=[pltpu.VMEM((1, n_exp), F32)],
        compiler_params=_params("arbitrary"),
    )(top_i)


def _row_copy(src_ref, src_row, dst_ref, dst_row, sem):
    return pltpu.make_async_copy(src_ref.at[pl.ds(src_row, 1), :], dst_ref.at[pl.ds(dst_row, 1), :], sem)


def _scatter_kernel(dest_ref, x_ref, init_ref, xs_ref, sem):
    del init_ref
    t = x_ref.shape[0]

    def issue(r, carry):
        for k in range(TOP_K):
            _row_copy(x_ref, r, xs_ref, dest_ref[r * TOP_K + k], sem).start()
        return carry

    lax.fori_loop(0, t, issue, 0)

    def drain(r, carry):
        for k in range(TOP_K):
            _row_copy(x_ref, r, xs_ref, dest_ref[r * TOP_K + k], sem).wait()
        return carry

    lax.fori_loop(0, t, drain, 0)


def _scatter_rows(dest_flat, packed, n_rows):
    n, w = packed.shape
    init = jnp.zeros((n_rows, w), packed.dtype)
    return pl.pallas_call(
        _scatter_kernel,
        grid=(n // ROUTE_TILE,),
        in_specs=[
            pl.BlockSpec((ROUTE_TILE * TOP_K,), lambda i: (i,), memory_space=pltpu.SMEM),
            pl.BlockSpec((ROUTE_TILE, w), lambda i: (i, 0)),
            pl.BlockSpec(memory_space=pl.ANY),
        ],
        out_specs=pl.BlockSpec(memory_space=pl.ANY),
        out_shape=jax.ShapeDtypeStruct((n_rows, w), packed.dtype),
        scratch_shapes=[pltpu.SemaphoreType.DMA(())],
        input_output_aliases={2: 0},
        compiler_params=_params("arbitrary"),
    )(dest_flat, packed, init)


def _ffn_kernel(te_ref, nv_ref, xs_ref, wg_ref, wl_ref, bg_ref, bl_ref, wd_ref, bd_ref, ys_ref):
    i = pl.program_id(0)

    @pl.when(i < nv_ref[0])
    def _():
        u = xs_ref[...]
        lo = lax.bitcast_convert_type(u << 16, F32).astype(BF16)
        hi = lax.bitcast_convert_type(u & jnp.uint32(0xFFFF0000), F32).astype(BF16)
        x = jnp.concatenate([lo, hi], axis=1)
        hg = jnp.dot(x, wg_ref[0], preferred_element_type=F32) + bg_ref[0]
        hl = jnp.dot(x, wl_ref[0], preferred_element_type=F32) + bl_ref[0]
        xg = jnp.minimum(hg, SWIGLU_LIMIT)
        xl = jnp.clip(hl, -SWIGLU_LIMIT, SWIGLU_LIMIT)
        act = xg * jax.nn.sigmoid(SWIGLU_ALPHA * xg) * (xl + 1.0)
        ys_ref[...] = jnp.dot(act.astype(BF16), wd_ref[0], preferred_element_type=F32) + bd_ref[0]

    @pl.when(i >= nv_ref[0])
    def _():
        ys_ref[...] = jnp.zeros_like(ys_ref)


def _grouped_ffn(tile_expert, n_valid, xs, wg, wl, bg, bl, wd, bd):
    n_rows, w = xs.shape
    n_exp, d, de = wg.shape
    n_tiles = n_rows // FFN_TILE
    row = lambda i, te, nv: (jnp.minimum(i, nv[0] - 1), 0)
    exp3 = lambda i, te, nv: (te[i], 0, 0)
    grid_spec = pltpu.PrefetchScalarGridSpec(
        num_scalar_prefetch=2,
        grid=(n_tiles,),
        in_specs=[
            pl.BlockSpec((FFN_TILE, w), row),
            pl.BlockSpec((1, d, de), exp3),
            pl.BlockSpec((1, d, de), exp3),
            pl.BlockSpec((1, 1, de), exp3),
            pl.BlockSpec((1, 1, de), exp3),
            pl.BlockSpec((1, de, d), exp3),
            pl.BlockSpec((1, 1, d), exp3),
        ],
        out_specs=pl.BlockSpec((FFN_TILE, d), lambda i, te, nv: (i, 0)),
    )
    return pl.pallas_call(
        _ffn_kernel,
        grid_spec=grid_spec,
        out_shape=jax.ShapeDtypeStruct((n_rows, d), F32),
        compiler_params=_params("arbitrary"),
    )(tile_expert, n_valid, xs, wg, wl, bg, bl, wd, bd)


def _combine_kernel(dest_ref, gt_ref, h1_ref, g_ref, b_ref, ys_ref, o_ref, buf_ref, sem):
    t = h1_ref.shape[0]

    def issue(r, carry):
        for k in range(TOP_K):
            _row_copy(ys_ref, dest_ref[r * TOP_K + k], buf_ref.at[k], r, sem).start()
        return carry

    lax.fori_loop(0, t, issue, 0)

    def drain(r, carry):
        for k in range(TOP_K):
            _row_copy(ys_ref, dest_ref[r * TOP_K + k], buf_ref.at[k], r, sem).wait()
        return carry

    lax.fori_loop(0, t, drain, 0)

    gates = gt_ref[...]
    ffn = gates[:, 0:1] * buf_ref[0]
    for k in range(1, TOP_K):
        ffn = ffn + gates[:, k:k + 1] * buf_ref[k]
    o_ref[...] = _layer_norm(DEEPNORM_ALPHA * h1_ref[...] + ffn, g_ref[...], b_ref[...])


def _combine(dest_flat, gates, h1, g, b, ys):
    n, d = h1.shape
    return pl.pallas_call(
        _combine_kernel,
        grid=(n // ROUTE_TILE,),
        in_specs=[
            pl.BlockSpec((ROUTE_TILE * TOP_K,), lambda i: (i,), memory_space=pltpu.SMEM),
            pl.BlockSpec((ROUTE_TILE, TOP_K), lambda i: (i, 0)),
            pl.BlockSpec((ROUTE_TILE, d), lambda i: (i, 0)),
            pl.BlockSpec((1, d), lambda i: (0, 0)),
            pl.BlockSpec((1, d), lambda i: (0, 0)),
            pl.BlockSpec(memory_space=pl.ANY),
        ],
        out_specs=pl.BlockSpec((ROUTE_TILE, d), lambda i: (i, 0)),
        out_shape=jax.ShapeDtypeStruct((n, d), F32),
        scratch_shapes=[pltpu.VMEM((TOP_K, ROUTE_TILE, d), F32), pltpu.SemaphoreType.DMA(())],
        compiler_params=_params("arbitrary"),
    )(dest_flat, gates, h1, g, b, ys)


def kernel(x, emb_ln_g, emb_ln_b, w_in, attn_sink, hgrn_lb_logits, hgrn_norm_g, w_out, ln1_g, ln1_b,
           w_router, b_router, w_up, b_up, w_down, b_down, ln2_g, ln2_b):
    bsz, seq, d = x.shape
    n = bsz * seq
    n_exp = w_router.shape[-1]
    assert w_in.shape[0] == DEPTH == 1
    assert seq % ATTN_BLOCK == 0 and seq % HGRN_CHUNK == 0
    assert n % TOKEN_TILE == 0 and n % ROUTE_TILE == 0 and (ROUTE_TILE * TOP_K) % 1024 == 0
    row = lambda v: v.reshape(1, -1).astype(F32)

    x2 = x.reshape(n, d)
    g0, b0 = row(emb_ln_g), row(emb_ln_b)

    proj = _inproj(x2, g0, b0, w_in[0].astype(BF16))
    proj3 = proj.reshape(bsz, seq, -1)
    attn = _attention(proj3, attn_sink[0].astype(F32))

    lb = jnp.cumsum(jax.nn.softmax(hgrn_lb_logits.astype(F32), axis=1), axis=1)[:, 0]
    rec = _hgrn(proj3, lb, row(hgrn_norm_g[0]))

    wr = w_router[0].astype(F32)
    wr_hi = wr.astype(BF16)
    wr_lo = (wr - wr_hi.astype(F32)).astype(BF16)
    h1, packed, top_i, gates = _outproj(
        x2, attn.reshape(n, -1), rec.reshape(n, -1), w_out[0].astype(BF16),
        g0, b0, row(ln1_g[0]), row(ln1_b[0]), jnp.stack([wr_hi, wr_lo]), row(b_router[0]))

    rank, counts = _ranks(top_i, n_exp)
    counts = counts[0]
    padded = ((counts + FFN_TILE - 1) // FFN_TILE) * FFN_TILE
    ends = jnp.cumsum(padded)
    starts = ends - padded
    n_tiles = (n * TOP_K) // FFN_TILE + n_exp
    n_rows = n_tiles * FFN_TILE
    dest_flat = (starts[top_i] + rank).reshape(-1)
    tile_start = jnp.arange(n_tiles, dtype=jnp.int32) * FFN_TILE
    tile_expert = jnp.minimum(
        jnp.sum(tile_start[:, None] >= ends[None, :], axis=1), n_exp - 1).astype(jnp.int32)
    n_valid = (ends[-1:] // FFN_TILE).astype(jnp.int32)

    xs = _scatter_rows(dest_flat, packed, n_rows)

    wu = w_up[0]
    bu = b_up[0].astype(F32)
    ys = _grouped_ffn(
        tile_expert, n_valid, xs,
        wu[:, :, 0::2].astype(BF16), wu[:, :, 1::2].astype(BF16),
        bu[:, None, 0::2], bu[:, None, 1::2],
        w_down[0].astype(BF16), b_down[0].astype(F32)[:, None, :])

    out = _combine(dest_flat, gates, h1, row(ln2_g[0]), row(ln2_b[0]), ys)
    return out.reshape(bsz, seq, d)
```

```python
import functools
import math

import jax
import jax.numpy as jnp
from jax import lax
from jax.experimental import pallas as pl
from jax.experimental.pallas import tpu as pltpu

F32 = jnp.float32
BF16 = jnp.bfloat16

ATTN_HEAD_DIM = 64
N_Q_HEADS = 8
N_KV_HEADS = 2
ATTN_WIDTH = N_Q_HEADS * ATTN_HEAD_DIM
KV_WIDTH = N_KV_HEADS * ATTN_HEAD_DIM
WINDOW = 128
ATTN_BLOCK = 128
HGRN_HEAD_DIM = 128
N_HGRN_HEADS = 4
HGRN_WIDTH = N_HGRN_HEADS * HGRN_HEAD_DIM
HGRN_CHUNK = 64
TOP_K = 4
SWIGLU_LIMIT = 7.0
SWIGLU_ALPHA = 1.702
LN_EPS = 1e-5
RMS_EPS = 1e-6
DEPTH = 1
DEEPNORM_ALPHA = (2.0 * DEPTH) ** 0.25

VMEM_LIMIT_BYTES = 48 * 1024 * 1024
TOKEN_TILE = 512
ROUTE_TILE = 256
FFN_TILE = 512
ROW_UNROLL = 8


def _params(*sem):
    return pltpu.CompilerParams(dimension_semantics=sem, vmem_limit_bytes=VMEM_LIMIT_BYTES)


def _layer_norm(x, g, b):
    mu = jnp.mean(x, axis=-1, keepdims=True)
    xc = x - mu
    var = jnp.mean(xc * xc, axis=-1, keepdims=True)
    return xc * lax.rsqrt(var + LN_EPS) * g + b


def _split3(x):
    hi = x.astype(BF16)
    r = x - hi.astype(F32)
    mid = r.astype(BF16)
    lo = (r - mid.astype(F32)).astype(BF16)
    return hi, mid, lo


def _inproj_kernel(x_ref, g_ref, b_ref, w_ref, o_ref, *, col_chunk):
    h = _layer_norm(x_ref[...], g_ref[...], b_ref[...]).astype(BF16)
    n_cols = o_ref.shape[1]
    for c in range(0, n_cols, col_chunk):
        o_ref[:, c:c + col_chunk] = jnp.dot(
            h, w_ref[:, c:c + col_chunk], preferred_element_type=F32).astype(o_ref.dtype)


def _inproj(x2, g, b, w_bf16):
    n, d = x2.shape
    cols = w_bf16.shape[1]
    return pl.pallas_call(
        functools.partial(_inproj_kernel, col_chunk=256),
        grid=(n // TOKEN_TILE,),
        in_specs=[
            pl.BlockSpec((TOKEN_TILE, d), lambda i: (i, 0)),
            pl.BlockSpec((1, d), lambda i: (0, 0)),
            pl.BlockSpec((1, d), lambda i: (0, 0)),
            pl.BlockSpec((d, cols), lambda i: (0, 0)),
        ],
        out_specs=pl.BlockSpec((TOKEN_TILE, cols), lambda i: (i, 0)),
        out_shape=jax.ShapeDtypeStruct((n, cols), BF16),
        compiler_params=_params("parallel"),
    )(x2, g, b, w_bf16)


def _attn_kernel(sink_ref, q_ref, kp_ref, ko_ref, kn_ref, o_ref, *, seq):
    n = pl.program_id(1)
    blk = ATTN_BLOCK
    q = q_ref[0]
    kv = jnp.concatenate([kp_ref[0], ko_ref[0], kn_ref[0]], axis=0)
    qpos = lax.broadcasted_iota(jnp.int32, (blk, 3 * blk), 0)
    krel = lax.broadcasted_iota(jnp.int32, (blk, 3 * blk), 1) - blk
    dist = jnp.abs(krel - qpos)
    kpos = krel + n * blk
    valid = (dist <= WINDOW) & (kpos >= 0) & (kpos < seq)
    distf = dist.astype(F32)
    scale = 1.0 / math.sqrt(ATTN_HEAD_DIM)
    rep = N_Q_HEADS // N_KV_HEADS
    outs = []
    for h in range(N_Q_HEADS):
        g = h // rep
        slope = 2.0 ** (-8.0 * (h + 1) / N_Q_HEADS)
        qh = q[:, h * ATTN_HEAD_DIM:(h + 1) * ATTN_HEAD_DIM]
        kg = kv[:, g * ATTN_HEAD_DIM:(g + 1) * ATTN_HEAD_DIM]
        vg = kv[:, KV_WIDTH + g * ATTN_HEAD_DIM:KV_WIDTH + (g + 1) * ATTN_HEAD_DIM]
        s = lax.dot_general(qh, kg, (((1,), (1,)), ((), ())), preferred_element_type=F32) * scale
        s = jnp.where(valid, s - slope * distf, -jnp.inf)
        sink = sink_ref[h]
        m = jnp.maximum(jnp.max(s, axis=-1, keepdims=True), sink)
        e = jnp.exp(s - m)
        denom = jnp.sum(e, axis=-1, keepdims=True) + jnp.exp(sink - m)
        pv = jnp.dot(e.astype(BF16), vg, preferred_element_type=F32)
        outs.append(pv / denom)
    o_ref[0] = jnp.concatenate(outs, axis=1).astype(o_ref.dtype)


def _attention(proj3, sink):
    b, s, _ = proj3.shape
    nb = s // ATTN_BLOCK
    kv_col = ATTN_WIDTH // (2 * KV_WIDTH)
    kv_spec = lambda f: pl.BlockSpec((1, ATTN_BLOCK, 2 * KV_WIDTH), f)
    return pl.pallas_call(
        functools.partial(_attn_kernel, seq=s),
        grid=(b, nb),
        in_specs=[
            pl.BlockSpec(memory_space=pltpu.SMEM),
            pl.BlockSpec((1, ATTN_BLOCK, ATTN_WIDTH), lambda i, j: (i, j, 0)),
            kv_spec(lambda i, j: (i, jnp.maximum(j - 1, 0), kv_col)),
            kv_spec(lambda i, j: (i, j, kv_col)),
            kv_spec(lambda i, j: (i, jnp.minimum(j + 1, nb - 1), kv_col)),
        ],
        out_specs=pl.BlockSpec((1, ATTN_BLOCK, ATTN_WIDTH), lambda i, j: (i, j, 0)),
        out_shape=jax.ShapeDtypeStruct((b, s, ATTN_WIDTH), BF16),
        compiler_params=_params("parallel", "parallel"),
    )(sink, proj3, proj3, proj3, proj3)


def _hgrn_chunk(q, v, z, lb, state_t, tri, forward):
    c = q.shape[0]
    dk = q.shape[1]
    f = lb + (1.0 - lb) * jax.nn.sigmoid(z)
    log_f = jnp.log(f)
    k = 1.0 - f
    pieces = jnp.concatenate(_split3(log_f), axis=1)
    cum3 = jnp.dot(tri, pieces, preferred_element_type=F32)
    cum = cum3[:, :dk] + cum3[:, dk:2 * dk] + cum3[:, 2 * dk:]
    if forward:
        ref = cum[c // 2:c // 2 + 1, :]
        last = cum[c - 1:c, :]
    else:
        ref = cum[c - 1 - c // 2:c - c // 2, :]
        last = cum[0:1, :]
    q_rel = (q * jnp.exp(cum - ref)).astype(BF16)
    k_rel = (k * jnp.exp(ref - cum)).astype(BF16)
    a = lax.dot_general(q_rel, k_rel, (((1,), (1,)), ((), ())), preferred_element_type=F32)
    a = jnp.where(tri > 0, a, 0.0).astype(BF16)
    o_intra = jnp.dot(a, v, preferred_element_type=F32)
    k_dec = (k * jnp.exp(last - cum)).astype(BF16)
    kv_t = lax.dot_general(v, k_dec, (((0,), (0,)), ((), ())), preferred_element_type=F32)
    q_dec = (q * jnp.exp(cum)).astype(BF16)
    o_inter = lax.dot_general(q_dec, state_t.astype(BF16), (((1,), (1,)), ((), ())),
                              preferred_element_type=F32)
    new_state_t = state_t * jnp.exp(last) + kv_t
    return o_intra + o_inter, new_state_t


def _hgrn_kernel(q_ref, v_ref, zf_ref, zb_ref, zg_ref, lb_ref, ng_ref, o_ref, of_ref, ob_ref):
    c = HGRN_CHUNK
    seq = q_ref.shape[1]
    n_chunks = seq // c
    row = lax.broadcasted_iota(jnp.int32, (c, c), 0)
    col = lax.broadcasted_iota(jnp.int32, (c, c), 1)
    lower = jnp.where(row >= col, 1.0, 0.0).astype(BF16)
    upper = jnp.where(row <= col, 1.0, 0.0).astype(BF16)
    lb_f = lb_ref[0:1, :]
    lb_b = lb_ref[1:2, :]

    def body(i, carry):
        st_f, st_b = carry
        sl_f = pl.ds(pl.multiple_of(i * c, c), c)
        sl_b = pl.ds(pl.multiple_of((n_chunks - 1 - i) * c, c), c)
        o_f, st_f = _hgrn_chunk(q_ref[0, sl_f, :].astype(F32), v_ref[0, sl_f, :],
                                zf_ref[0, sl_f, :].astype(F32), lb_f, st_f, lower, True)
        o_b, st_b = _hgrn_chunk(q_ref[0, sl_b, :].astype(F32), v_ref[0, sl_b, :],
                                zb_ref[0, sl_b, :].astype(F32), lb_b, st_b, upper, False)
        of_ref[sl_f, :] = o_f
        ob_ref[sl_b, :] = o_b
        return st_f, st_b

    zero = jnp.zeros((v_ref.shape[2], q_ref.shape[2]), F32)
    lax.fori_loop(0, n_chunks, body, (zero, zero))

    o = of_ref[...] + ob_ref[...]
    o = o * lax.rsqrt(jnp.mean(o * o, axis=-1, keepdims=True) + RMS_EPS)
    zg = zg_ref[0].astype(F32)
    o_ref[0] = (o * ng_ref[...] * (zg * jax.nn.sigmoid(zg))).astype(o_ref.dtype)


def _hgrn(proj3, lb, norm_g):
    b, s, _ = proj3.shape
    hd = HGRN_HEAD_DIM
    base = (ATTN_WIDTH + 2 * KV_WIDTH) // hd
    col_spec = lambda part: pl.BlockSpec((1, s, hd), lambda i, h: (i, 0, base + part * N_HGRN_HEADS + h))
    return pl.pallas_call(
        _hgrn_kernel,
        grid=(b, N_HGRN_HEADS),
        in_specs=[
            col_spec(0), col_spec(1), col_spec(2), col_spec(3), col_spec(4),
            pl.BlockSpec((2, hd), lambda i, h: (0, h)),
            pl.BlockSpec((1, hd), lambda i, h: (0, h)),
        ],
        out_specs=pl.BlockSpec((1, s, hd), lambda i, h: (i, 0, h)),
        out_shape=jax.ShapeDtypeStruct((b, s, HGRN_WIDTH), BF16),
        scratch_shapes=[pltpu.VMEM((s, hd), F32), pltpu.VMEM((s, hd), F32)],
        compiler_params=_params("parallel", "parallel"),
    )(proj3, proj3, proj3, proj3, proj3, lb, norm_g)


def _outproj_kernel(x_ref, attn_ref, rec_ref, wo_ref, g0_ref, b0_ref, g1_ref, b1_ref,
                    wr_ref, br_ref, h1_ref, pk_ref, ti_ref, gt_ref):
    half = attn_ref.shape[1]
    h0 = _layer_norm(x_ref[...], g0_ref[...], b0_ref[...])
    mix = (jnp.dot(attn_ref[...], wo_ref[:half, :], preferred_element_type=F32)
           + jnp.dot(rec_ref[...], wo_ref[half:, :], preferred_element_type=F32))
    h1 = _layer_norm(DEEPNORM_ALPHA * h0 + mix, g1_ref[...], b1_ref[...])
    h1_ref[...] = h1

    d2 = h1.shape[1] // 2
    lo_bits = lax.bitcast_convert_type(h1[:, :d2].astype(BF16).astype(F32), jnp.uint32)
    hi_bits = lax.bitcast_convert_type(h1[:, d2:].astype(BF16).astype(F32), jnp.uint32)
    pk_ref[...] = hi_bits | (lo_bits >> 16)

    h_hi = h1.astype(BF16)
    h_lo = (h1 - h_hi.astype(F32)).astype(BF16)
    w_hi = wr_ref[0]
    w_lo = wr_ref[1]
    logits = (jnp.dot(h_hi, w_hi, preferred_element_type=F32)
              + jnp.dot(h_hi, w_lo, preferred_element_type=F32)
              + jnp.dot(h_lo, w_hi, preferred_element_type=F32)) + br_ref[...]

    n_exp = logits.shape[1]
    lane = lax.broadcasted_iota(jnp.int32, logits.shape, 1)
    vals, idxs = [], []
    cur = logits
    for _ in range(TOP_K):
        m = jnp.max(cur, axis=-1, keepdims=True)
        idx = jnp.min(jnp.where(cur == m, lane, n_exp), axis=-1, keepdims=True)
        vals.append(m)
        idxs.append(idx)
        cur = jnp.where(lane == idx, -jnp.inf, cur)
    top_v = jnp.concatenate(vals, axis=1)
    e = jnp.exp(top_v - vals[0])
    gt_ref[...] = e / jnp.sum(e, axis=-1, keepdims=True)
    ti_ref[...] = jnp.concatenate(idxs, axis=1)


def _outproj(x2, attn2, rec2, wo_bf16, g0, b0, g1, b1, wr2, br):
    n, d = x2.shape
    half = attn2.shape[1]
    n_exp = br.shape[1]
    tile = lambda w: pl.BlockSpec((TOKEN_TILE, w), lambda i: (i, 0))
    vec = pl.BlockSpec((1, d), lambda i: (0, 0))
    return pl.pallas_call(
        _outproj_kernel,
        grid=(n // TOKEN_TILE,),
        in_specs=[
            tile(d), tile(half), tile(half),
            pl.BlockSpec((d, d), lambda i: (0, 0)),
            vec, vec, vec, vec,
            pl.BlockSpec((2, d, n_exp), lambda i: (0, 0, 0)),
            pl.BlockSpec((1, n_exp), lambda i: (0, 0)),
        ],
        out_specs=[tile(d), tile(d // 2), tile(TOP_K), tile(TOP_K)],
        out_shape=[
            jax.ShapeDtypeStruct((n, d), F32),
            jax.ShapeDtypeStruct((n, d // 2), jnp.uint32),
            jax.ShapeDtypeStruct((n, TOP_K), jnp.int32),
            jax.ShapeDtypeStruct((n, TOP_K), F32),
        ],
        compiler_params=_params("parallel"),
    )(x2, attn2, rec2, wo_bf16, g0, b0, g1, b1, wr2, br)


def _rank_kernel(ti_ref, rank_ref, cnt_ref, run_ref, *, n_exp):
    i = pl.program_id(0)

    @pl.when(i == 0)
    def _():
        run_ref[...] = jnp.zeros_like(run_ref)

    ti = ti_ref[...]
    t = ti.shape[0]
    lane = lax.broadcasted_iota(jnp.int32, (t, n_exp), 1)
    hots = [jnp.where(lane == ti[:, k:k + 1], 1.0, 0.0) for k in range(TOP_K)]
    member = hots[0]
    for k in range(1, TOP_K):
        member = member + hots[k]
    row = lax.broadcasted_iota(jnp.int32, (t, t), 0)
    col = lax.broadcasted_iota(jnp.int32, (t, t), 1)
    strict_lower = jnp.where(row > col, 1.0, 0.0).astype(BF16)
    before = jnp.dot(strict_lower, member.astype(BF16), preferred_element_type=F32) + run_ref[...]
    ranks = [jnp.sum(hots[k] * before, axis=-1, keepdims=True) for k in range(TOP_K)]
    rank_ref[...] = jnp.concatenate(ranks, axis=1).astype(jnp.int32)
    run_ref[...] = run_ref[...] + jnp.sum(member, axis=0, keepdims=True)
    cnt_ref[...] = run_ref[...].astype(jnp.int32)


def _ranks(top_i, n_exp):
    n = top_i.shape[0]
    return pl.pallas_call(
        functools.partial(_rank_kernel, n_exp=n_exp),
        grid=(n // TOKEN_TILE,),
        in_specs=[pl.BlockSpec((TOKEN_TILE, TOP_K), lambda i: (i, 0))],
        out_specs=[pl.BlockSpec((TOKEN_TILE, TOP_K), lambda i: (i, 0)),
                   pl.BlockSpec((1, n_exp), lambda i: (0, 0))],
        out_shape=[jax.ShapeDtypeStruct((n, TOP_K), jnp.int32),
                   jax.ShapeDtypeStruct((1, n_exp), jnp.int32)],
        scratch_shapes=[pltpu.VMEM((1, n_exp), F32)],
        compiler_params=_params("arbitrary"),
    )(top_i)


def _row_copy(src_ref, src_row, dst_ref, dst_row, sem):
    return pltpu.make_async_copy(src_ref.at[pl.ds(src_row, 1), :], dst_ref.at[pl.ds(dst_row, 1), :], sem)


def _scatter_kernel(dest_ref, x_ref, init_ref, xs_ref, sem):
    del init_ref
    t = x_ref.shape[0]

    def copies(i):
        for u in range(ROW_UNROLL):
            r = i * ROW_UNROLL + u
            for k in range(TOP_K):
                yield _row_copy(x_ref, r, xs_ref, dest_ref[r * TOP_K + k], sem), (u * TOP_K + k) % 2

    def issue(i, carry):
        for cp, prio in copies(i):
            cp.start(priority=prio)
        return carry

    def drain(i, carry):
        for cp, _ in copies(i):
            cp.wait()
        return carry

    lax.fori_loop(0, t // ROW_UNROLL, issue, 0)
    lax.fori_loop(0, t // ROW_UNROLL, drain, 0)


def _scatter_rows(dest_flat, packed, n_rows):
    n, w = packed.shape
    init = jnp.zeros((n_rows, w), packed.dtype)
    return pl.pallas_call(
        _scatter_kernel,
        grid=(n // ROUTE_TILE,),
        in_specs=[
            pl.BlockSpec((ROUTE_TILE * TOP_K,), lambda i: (i,), memory_space=pltpu.SMEM),
            pl.BlockSpec((ROUTE_TILE, w), lambda i: (i, 0)),
            pl.BlockSpec(memory_space=pl.ANY),
        ],
        out_specs=pl.BlockSpec(memory_space=pl.ANY),
        out_shape=jax.ShapeDtypeStruct((n_rows, w), packed.dtype),
        scratch_shapes=[pltpu.SemaphoreType.DMA(())],
        input_output_aliases={2: 0},
        compiler_params=_params("arbitrary"),
    )(dest_flat, packed, init)


def _split_up_kernel(w_ref, wg_ref, wl_ref, *, blk):
    r = lax.broadcasted_iota(jnp.int32, (2 * blk, blk), 0)
    c = lax.broadcasted_iota(jnp.int32, (2 * blk, blk), 1)
    pick_even = jnp.where(r == 2 * c, 1.0, 0.0).astype(BF16)
    pick_odd = jnp.where(r == 2 * c + 1, 1.0, 0.0).astype(BF16)
    for j in range(wg_ref.shape[2] // blk):
        w = w_ref[0, :, 2 * blk * j:2 * blk * (j + 1)].astype(BF16)
        wg_ref[0, :, blk * j:blk * (j + 1)] = jnp.dot(w, pick_even, preferred_element_type=F32).astype(BF16)
        wl_ref[0, :, blk * j:blk * (j + 1)] = jnp.dot(w, pick_odd, preferred_element_type=F32).astype(BF16)


def _split_up_weights(w_up):
    n_exp, d, de2 = w_up.shape
    de = de2 // 2
    out = jax.ShapeDtypeStruct((n_exp, d, de), BF16)
    return pl.pallas_call(
        functools.partial(_split_up_kernel, blk=256),
        grid=(n_exp,),
        in_specs=[pl.BlockSpec((1, d, de2), lambda e: (e, 0, 0))],
        out_specs=[pl.BlockSpec((1, d, de), lambda e: (e, 0, 0)),
                   pl.BlockSpec((1, d, de), lambda e: (e, 0, 0))],
        out_shape=[out, out],
        compiler_params=_params("parallel"),
    )(w_up)


def _ffn_kernel(te_ref, nv_ref, xs_ref, wg_ref, wl_ref, bg_ref, bl_ref, wd_ref, bd_ref, ys_ref):
    i = pl.program_id(0)

    @pl.when(i < nv_ref[0])
    def _():
        u = xs_ref[...]
        lo = lax.bitcast_convert_type(u << 16, F32).astype(BF16)
        hi = lax.bitcast_convert_type(u & jnp.uint32(0xFFFF0000), F32).astype(BF16)
        x = jnp.concatenate([lo, hi], axis=1)
        hg = jnp.dot(x, wg_ref[0], preferred_element_type=F32) + bg_ref[0]
        hl = jnp.dot(x, wl_ref[0], preferred_element_type=F32) + bl_ref[0]
        xg = jnp.minimum(hg, SWIGLU_LIMIT)
        xl = jnp.clip(hl, -SWIGLU_LIMIT, SWIGLU_LIMIT)
        act = xg * jax.nn.sigmoid(SWIGLU_ALPHA * xg) * (xl + 1.0)
        ys_ref[...] = jnp.dot(act.astype(BF16), wd_ref[0], preferred_element_type=F32) + bd_ref[0]

    @pl.when(i >= nv_ref[0])
    def _():
        ys_ref[...] = jnp.zeros_like(ys_ref)


def _grouped_ffn(tile_expert, n_valid, xs, wg, wl, bg, bl, wd, bd):
    n_rows, w = xs.shape
    n_exp, d, de = wg.shape
    n_tiles = n_rows // FFN_TILE
    row = lambda i, te, nv: (jnp.minimum(i, nv[0] - 1), 0)
    exp3 = lambda i, te, nv: (te[i], 0, 0)
    grid_spec = pltpu.PrefetchScalarGridSpec(
        num_scalar_prefetch=2,
        grid=(n_tiles,),
        in_specs=[
            pl.BlockSpec((FFN_TILE, w), row),
            pl.BlockSpec((1, d, de), exp3),
            pl.BlockSpec((1, d, de), exp3),
            pl.BlockSpec((1, 1, de), exp3),
            pl.BlockSpec((1, 1, de), exp3),
            pl.BlockSpec((1, de, d), exp3),
            pl.BlockSpec((1, 1, d), exp3),
        ],
        out_specs=pl.BlockSpec((FFN_TILE, d), lambda i, te, nv: (i, 0)),
    )
    return pl.pallas_call(
        _ffn_kernel,
        grid_spec=grid_spec,
        out_shape=jax.ShapeDtypeStruct((n_rows, d), F32),
        compiler_params=_params("arbitrary"),
    )(tile_expert, n_valid, xs, wg, wl, bg, bl, wd, bd)


def _combine_kernel(dest_ref, gt_ref, h1_ref, g_ref, b_ref, ys_ref, o_ref, buf_ref, sem):
    t = h1_ref.shape[0]

    def copies(i):
        for u in range(ROW_UNROLL):
            r = i * ROW_UNROLL + u
            for k in range(TOP_K):
                yield _row_copy(ys_ref, dest_ref[r * TOP_K + k], buf_ref.at[k], r, sem), (u * TOP_K + k) % 2

    def issue(i, carry):
        for cp, prio in copies(i):
            cp.start(priority=prio)
        return carry

    def drain(i, carry):
        for cp, _ in copies(i):
            cp.wait()
        return carry

    lax.fori_loop(0, t // ROW_UNROLL, issue, 0)
    lax.fori_loop(0, t // ROW_UNROLL, drain, 0)

    gates = gt_ref[...]
    ffn = gates[:, 0:1] * buf_ref[0]
    for k in range(1, TOP_K):
        ffn = ffn + gates[:, k:k + 1] * buf_ref[k]
    o_ref[...] = _layer_norm(DEEPNORM_ALPHA * h1_ref[...] + ffn, g_ref[...], b_ref[...])


def _combine(dest_flat, gates, h1, g, b, ys):
    n, d = h1.shape
    return pl.pallas_call(
        _combine_kernel,
        grid=(n // ROUTE_TILE,),
        in_specs=[
            pl.BlockSpec((ROUTE_TILE * TOP_K,), lambda i: (i,), memory_space=pltpu.SMEM),
            pl.BlockSpec((ROUTE_TILE, TOP_K), lambda i: (i, 0)),
            pl.BlockSpec((ROUTE_TILE, d), lambda i: (i, 0)),
            pl.BlockSpec((1, d), lambda i: (0, 0)),
            pl.BlockSpec((1, d), lambda i: (0, 0)),
            pl.BlockSpec(memory_space=pl.ANY),
        ],
        out_specs=pl.BlockSpec((ROUTE_TILE, d), lambda i: (i, 0)),
        out_shape=jax.ShapeDtypeStruct((n, d), F32),
        scratch_shapes=[pltpu.VMEM((TOP_K, ROUTE_TILE, d), F32), pltpu.SemaphoreType.DMA(())],
        compiler_params=_params("arbitrary"),
    )(dest_flat, gates, h1, g, b, ys)


def kernel(x, emb_ln_g, emb_ln_b, w_in, attn_sink, hgrn_lb_logits, hgrn_norm_g, w_out, ln1_g, ln1_b,
           w_router, b_router, w_up, b_up, w_down, b_down, ln2_g, ln2_b):
    bsz, seq, d = x.shape
    n = bsz * seq
    n_exp = w_router.shape[-1]
    assert w_in.shape[0] == DEPTH == 1
    assert seq % ATTN_BLOCK == 0 and seq % HGRN_CHUNK == 0
    assert n % TOKEN_TILE == 0 and n % ROUTE_TILE == 0 and (ROUTE_TILE * TOP_K) % 1024 == 0
    row = lambda v: v.reshape(1, -1).astype(F32)

    x2 = x.reshape(n, d)
    g0, b0 = row(emb_ln_g), row(emb_ln_b)

    proj = _inproj(x2, g0, b0, w_in[0].astype(BF16))
    proj3 = proj.reshape(bsz, seq, -1)
    attn = _attention(proj3, attn_sink[0].astype(F32))

    lb = jnp.cumsum(jax.nn.softmax(hgrn_lb_logits.astype(F32), axis=1), axis=1)[:, 0]
    rec = _hgrn(proj3, lb, row(hgrn_norm_g[0]))

    wr = w_router[0].astype(F32)
    wr_hi = wr.astype(BF16)
    wr_lo = (wr - wr_hi.astype(F32)).astype(BF16)
    h1, packed, top_i, gates = _outproj(
        x2, attn.reshape(n, -1), rec.reshape(n, -1), w_out[0].astype(BF16),
        g0, b0, row(ln1_g[0]), row(ln1_b[0]), jnp.stack([wr_hi, wr_lo]), row(b_router[0]))

    rank, counts = _ranks(top_i, n_exp)
    counts = counts[0]
    padded = ((counts + FFN_TILE - 1) // FFN_TILE) * FFN_TILE
    ends = jnp.cumsum(padded)
    starts = ends - padded
    n_tiles = (n * TOP_K) // FFN_TILE + n_exp
    n_rows = n_tiles * FFN_TILE
    dest_flat = (starts[top_i] + rank).reshape(-1)
    tile_start = jnp.arange(n_tiles, dtype=jnp.int32) * FFN_TILE
    tile_expert = jnp.minimum(
        jnp.sum(tile_start[:, None] >= ends[None, :], axis=1), n_exp - 1).astype(jnp.int32)
    n_valid = (ends[-1:] // FFN_TILE).astype(jnp.int32)

    xs = _scatter_rows(dest_flat, packed, n_rows)

    wg, wl = _split_up_weights(w_up[0].astype(F32))
    bu = b_up[0].astype(F32)
    ys = _grouped_ffn(
        tile_expert, n_valid, xs, wg, wl,
        bu[:, None, 0::2], bu[:, None, 1::2],
        w_down[0].astype(BF16), b_down[0].astype(F32)[:, None, :])

    out = _combine(dest_flat, gates, h1, row(ln2_g[0]), row(ln2_b[0]), ys)
    return out.reshape(bsz, seq, d)
```

```python
import functools
import math

import jax
import jax.numpy as jnp
from jax import lax
from jax.experimental import pallas as pl
from jax.experimental.pallas import tpu as pltpu

F32 = jnp.float32
BF16 = jnp.bfloat16

ATTN_HEAD_DIM = 64
N_Q_HEADS = 8
N_KV_HEADS = 2
ATTN_WIDTH = N_Q_HEADS * ATTN_HEAD_DIM
KV_WIDTH = N_KV_HEADS * ATTN_HEAD_DIM
WINDOW = 128
ATTN_BLOCK = 128
HGRN_HEAD_DIM = 128
N_HGRN_HEADS = 4
HGRN_WIDTH = N_HGRN_HEADS * HGRN_HEAD_DIM
HGRN_CHUNK = 64
TOP_K = 4
SWIGLU_LIMIT = 7.0
SWIGLU_ALPHA = 1.702
LN_EPS = 1e-5
RMS_EPS = 1e-6
DEPTH = 1
DEEPNORM_ALPHA = (2.0 * DEPTH) ** 0.25

VMEM_LIMIT_BYTES = 48 * 1024 * 1024
TOKEN_TILE = 512
ROUTE_TILE = 256
FFN_TILE = 512
ROW_UNROLL = 256
HGRN_GROUP = 16


def _params(*sem):
    return pltpu.CompilerParams(dimension_semantics=sem, vmem_limit_bytes=VMEM_LIMIT_BYTES)


def _layer_norm(x, g, b):
    mu = jnp.mean(x, axis=-1, keepdims=True)
    xc = x - mu
    var = jnp.mean(xc * xc, axis=-1, keepdims=True)
    return xc * lax.rsqrt(var + LN_EPS) * g + b


def _split3(x):
    hi = x.astype(BF16)
    r = x - hi.astype(F32)
    mid = r.astype(BF16)
    lo = (r - mid.astype(F32)).astype(BF16)
    return hi, mid, lo


def _inproj_kernel(x_ref, g_ref, b_ref, w_ref, o_ref, *, col_chunk):
    h = _layer_norm(x_ref[...], g_ref[...], b_ref[...]).astype(BF16)
    n_cols = o_ref.shape[1]
    for c in range(0, n_cols, col_chunk):
        o_ref[:, c:c + col_chunk] = jnp.dot(
            h, w_ref[:, c:c + col_chunk], preferred_element_type=F32).astype(o_ref.dtype)


def _inproj(x2, g, b, w_bf16):
    n, d = x2.shape
    cols = w_bf16.shape[1]
    return pl.pallas_call(
        functools.partial(_inproj_kernel, col_chunk=256),
        grid=(n // TOKEN_TILE,),
        in_specs=[
            pl.BlockSpec((TOKEN_TILE, d), lambda i: (i, 0)),
            pl.BlockSpec((1, d), lambda i: (0, 0)),
            pl.BlockSpec((1, d), lambda i: (0, 0)),
            pl.BlockSpec((d, cols), lambda i: (0, 0)),
        ],
        out_specs=pl.BlockSpec((TOKEN_TILE, cols), lambda i: (i, 0)),
        out_shape=jax.ShapeDtypeStruct((n, cols), BF16),
        compiler_params=_params("parallel"),
    )(x2, g, b, w_bf16)


def _attn_kernel(sink_ref, bias_ref, q_ref, kp_ref, ko_ref, kn_ref, o_ref):
    blk, hd = ATTN_BLOCK, ATTN_HEAD_DIM
    rep = N_Q_HEADS // N_KV_HEADS
    q = q_ref[0] * (1.0 / math.sqrt(hd))
    kv = jnp.concatenate([kp_ref[0], ko_ref[0], kn_ref[0]], axis=0)
    ones = jnp.ones((8, 3 * blk), BF16)
    outs = []
    for g in range(N_KV_HEADS):
        heads = range(g * rep, (g + 1) * rep)
        qg = jnp.concatenate([q[:, h * hd:(h + 1) * hd] for h in heads], axis=0)
        kg = kv[:, g * hd:(g + 1) * hd]
        vg = kv[:, KV_WIDTH + g * hd:KV_WIDTH + (g + 1) * hd]
        st = lax.dot_general(kg, qg, (((1,), (1,)), ((), ())), preferred_element_type=F32) + bias_ref[0, g]
        sink = jnp.concatenate([jnp.full((1, blk), sink_ref[h], F32) for h in heads], axis=1)
        m = jnp.maximum(jnp.max(st, axis=0, keepdims=True), sink)
        e = jnp.exp(st - m).astype(BF16)
        pv = lax.dot_general(vg, e, (((0,), (0,)), ((), ())), preferred_element_type=F32)
        den = jnp.dot(ones, e, preferred_element_type=F32)[0:1] + jnp.exp(sink - m)
        og = pv / den
        outs.extend(og[:, r * blk:(r + 1) * blk].T for r in range(rep))
    o_ref[0] = jnp.concatenate(outs, axis=1).astype(o_ref.dtype)


def _alibi_window_bias():
    blk = ATTN_BLOCK
    rep = N_Q_HEADS // N_KV_HEADS
    krel = jnp.arange(3 * blk)[:, None] - blk
    qpos = jnp.arange(blk)[None, :]
    dist = jnp.abs(krel - qpos)
    slopes = jnp.asarray([2.0 ** (-8.0 * (h + 1) / N_Q_HEADS) for h in range(N_Q_HEADS)], F32)
    bias = jnp.where(dist <= WINDOW, -slopes[:, None, None] * dist.astype(F32), -jnp.inf)
    bias = bias.reshape(N_KV_HEADS, rep, 3 * blk, blk).transpose(0, 2, 1, 3).reshape(N_KV_HEADS, 3 * blk, rep * blk)
    no_prev = jnp.where(krel >= 0, 0.0, -jnp.inf)
    no_next = jnp.where(krel < blk, 0.0, -jnp.inf)
    return jnp.stack([bias, bias + no_prev, bias + no_next, bias + no_prev + no_next])


def _attention(proj3, sink):
    b, s, _ = proj3.shape
    nb = s // ATTN_BLOCK
    kv_col = ATTN_WIDTH // (2 * KV_WIDTH)
    kv_spec = lambda f: pl.BlockSpec((1, ATTN_BLOCK, 2 * KV_WIDTH), f)
    bias = _alibi_window_bias()
    variant = lambda i, j: (jnp.where(j == 0, 1, 0) + jnp.where(j == nb - 1, 2, 0), 0, 0, 0)
    return pl.pallas_call(
        _attn_kernel,
        grid=(b, nb),
        in_specs=[
            pl.BlockSpec(memory_space=pltpu.SMEM),
            pl.BlockSpec((1,) + bias.shape[1:], variant),
            pl.BlockSpec((1, ATTN_BLOCK, ATTN_WIDTH), lambda i, j: (i, j, 0)),
            kv_spec(lambda i, j: (i, jnp.maximum(j - 1, 0), kv_col)),
            kv_spec(lambda i, j: (i, j, kv_col)),
            kv_spec(lambda i, j: (i, jnp.minimum(j + 1, nb - 1), kv_col)),
        ],
        out_specs=pl.BlockSpec((1, ATTN_BLOCK, ATTN_WIDTH), lambda i, j: (i, j, 0)),
        out_shape=jax.ShapeDtypeStruct((b, s, ATTN_WIDTH), BF16),
        compiler_params=_params("parallel", "parallel"),
    )(sink, bias, proj3, proj3, proj3, proj3)


def _hgrn_group(q, v, z, lb, state_t, tri, forward):
    g, c, dk = q.shape
    f = lb + (1.0 - lb) * jax.nn.sigmoid(z)
    log_f = jnp.log(f)
    k = 1.0 - f
    tri_g = jnp.broadcast_to(tri[None], (g, c, c))
    pieces = jnp.concatenate(_split3(log_f), axis=2)
    cum3 = jnp.einsum("gts,gsd->gtd", tri_g, pieces, preferred_element_type=F32)
    cum = cum3[:, :, :dk] + cum3[:, :, dk:2 * dk] + cum3[:, :, 2 * dk:]
    if forward:
        ref = cum[:, c // 2:c // 2 + 1, :]
        last = cum[:, c - 1:c, :]
    else:
        ref = cum[:, c - 1 - c // 2:c - c // 2, :]
        last = cum[:, 0:1, :]
    q_rel = (q * jnp.exp(cum - ref)).astype(BF16)
    k_rel = (k * jnp.exp(ref - cum)).astype(BF16)
    a = jnp.einsum("gtd,gsd->gts", q_rel, k_rel, preferred_element_type=F32)
    a = jnp.where(tri_g > 0, a, 0.0).astype(BF16)
    o_intra = jnp.einsum("gts,gsv->gtv", a, v, preferred_element_type=F32)
    k_dec = (k * jnp.exp(last - cum)).astype(BF16)
    kv_t = jnp.einsum("gsv,gsd->gvd", v, k_dec, preferred_element_type=F32)
    q_dec = (q * jnp.exp(cum)).astype(BF16)
    decay = jnp.exp(last)
    o_inter = [None] * g
    for j in (range(g) if forward else range(g - 1, -1, -1)):
        o_inter[j] = lax.dot_general(q_dec[j], state_t.astype(BF16), (((1,), (1,)), ((), ())),
                                     preferred_element_type=F32)
        state_t = state_t * decay[j] + kv_t[j]
    return o_intra + jnp.stack(o_inter, axis=0), state_t


def _hgrn_kernel(q_ref, v_ref, zf_ref, zb_ref, zg_ref, lb_ref, ng_ref, o_ref, of_ref, ob_ref):
    c = HGRN_CHUNK
    g = HGRN_GROUP
    seq, dk = q_ref.shape[1], q_ref.shape[2]
    dv = v_ref.shape[2]
    n_groups = seq // (g * c)
    row = lax.broadcasted_iota(jnp.int32, (c, c), 0)
    col = lax.broadcasted_iota(jnp.int32, (c, c), 1)
    lower = jnp.where(row >= col, 1.0, 0.0).astype(BF16)
    upper = jnp.where(row <= col, 1.0, 0.0).astype(BF16)
    lb_f = lb_ref[0:1, :]
    lb_b = lb_ref[1:2, :]

    def group(ref, sl, width):
        return ref[0, sl, :].astype(F32).reshape(g, c, width)

    def body(i, carry):
        st_f, st_b = carry
        sl_f = pl.ds(pl.multiple_of(i * (g * c), g * c), g * c)
        sl_b = pl.ds(pl.multiple_of((n_groups - 1 - i) * (g * c), g * c), g * c)
        o_f, st_f = _hgrn_group(group(q_ref, sl_f, dk), v_ref[0, sl_f, :].reshape(g, c, dv),
                                group(zf_ref, sl_f, dk), lb_f, st_f, lower, True)
        o_b, st_b = _hgrn_group(group(q_ref, sl_b, dk), v_ref[0, sl_b, :].reshape(g, c, dv),
                                group(zb_ref, sl_b, dk), lb_b, st_b, upper, False)
        of_ref[sl_f, :] = o_f.reshape(g * c, dv)
        ob_ref[sl_b, :] = o_b.reshape(g * c, dv)
        return st_f, st_b

    zero = jnp.zeros((dv, dk), F32)
    lax.fori_loop(0, n_groups, body, (zero, zero))

    o = of_ref[...] + ob_ref[...]
    o = o * lax.rsqrt(jnp.mean(o * o, axis=-1, keepdims=True) + RMS_EPS)
    zg = zg_ref[0].astype(F32)
    o_ref[0] = (o * ng_ref[...] * (zg * jax.nn.sigmoid(zg))).astype(o_ref.dtype)


def _hgrn(proj3, lb, norm_g):
    b, s, _ = proj3.shape
    hd = HGRN_HEAD_DIM
    base = (ATTN_WIDTH + 2 * KV_WIDTH) // hd
    col_spec = lambda part: pl.BlockSpec((1, s, hd), lambda i, h: (i, 0, base + part * N_HGRN_HEADS + h))
    return pl.pallas_call(
        _hgrn_kernel,
        grid=(b, N_HGRN_HEADS),
        in_specs=[
            col_spec(0), col_spec(1), col_spec(2), col_spec(3), col_spec(4),
            pl.BlockSpec((2, hd), lambda i, h: (0, h)),
            pl.BlockSpec((1, hd), lambda i, h: (0, h)),
        ],
        out_specs=pl.BlockSpec((1, s, hd), lambda i, h: (i, 0, h)),
        out_shape=jax.ShapeDtypeStruct((b, s, HGRN_WIDTH), BF16),
        scratch_shapes=[pltpu.VMEM((s, hd), F32), pltpu.VMEM((s, hd), F32)],
        compiler_params=_params("parallel", "parallel"),
    )(proj3, proj3, proj3, proj3, proj3, lb, norm_g)


def _outproj_kernel(x_ref, attn_ref, rec_ref, wo_ref, g0_ref, b0_ref, g1_ref, b1_ref,
                    wr_ref, br_ref, h1_ref, pk_ref, ti_ref, gt_ref):
    half = attn_ref.shape[1]
    h0 = _layer_norm(x_ref[...], g0_ref[...], b0_ref[...])
    mix = (jnp.dot(attn_ref[...], wo_ref[:half, :], preferred_element_type=F32)
           + jnp.dot(rec_ref[...], wo_ref[half:, :], preferred_element_type=F32))
    h1 = _layer_norm(DEEPNORM_ALPHA * h0 + mix, g1_ref[...], b1_ref[...])
    h1_ref[...] = h1

    d2 = h1.shape[1] // 2
    lo_bits = lax.bitcast_convert_type(h1[:, :d2].astype(BF16).astype(F32), jnp.uint32)
    hi_bits = lax.bitcast_convert_type(h1[:, d2:].astype(BF16).astype(F32), jnp.uint32)
    pk_ref[...] = hi_bits | (lo_bits >> 16)

    h_hi = h1.astype(BF16)
    h_lo = (h1 - h_hi.astype(F32)).astype(BF16)
    w_hi = wr_ref[0]
    w_lo = wr_ref[1]
    logits = (jnp.dot(h_hi, w_hi, preferred_element_type=F32)
              + jnp.dot(h_hi, w_lo, preferred_element_type=F32)
              + jnp.dot(h_lo, w_hi, preferred_element_type=F32)) + br_ref[...]

    n_exp = logits.shape[1]
    lane = lax.broadcasted_iota(jnp.int32, logits.shape, 1)
    vals, idxs = [], []
    cur = logits
    for _ in range(TOP_K):
        m = jnp.max(cur, axis=-1, keepdims=True)
        idx = jnp.min(jnp.where(cur == m, lane, n_exp), axis=-1, keepdims=True)
        vals.append(m)
        idxs.append(idx)
        cur = jnp.where(lane == idx, -jnp.inf, cur)
    top_v = jnp.concatenate(vals, axis=1)
    e = jnp.exp(top_v - vals[0])
    gt_ref[...] = e / jnp.sum(e, axis=-1, keepdims=True)
    ti_ref[...] = jnp.concatenate(idxs, axis=1)


def _outproj(x2, attn2, rec2, wo_bf16, g0, b0, g1, b1, wr2, br):
    n, d = x2.shape
    half = attn2.shape[1]
    n_exp = br.shape[1]
    tile = lambda w: pl.BlockSpec((TOKEN_TILE, w), lambda i: (i, 0))
    vec = pl.BlockSpec((1, d), lambda i: (0, 0))
    return pl.pallas_call(
        _outproj_kernel,
        grid=(n // TOKEN_TILE,),
        in_specs=[
            tile(d), tile(half), tile(half),
            pl.BlockSpec((d, d), lambda i: (0, 0)),
            vec, vec, vec, vec,
            pl.BlockSpec((2, d, n_exp), lambda i: (0, 0, 0)),
            pl.BlockSpec((1, n_exp), lambda i: (0, 0)),
        ],
        out_specs=[tile(d), tile(d // 2), tile(TOP_K), tile(TOP_K)],
        out_shape=[
            jax.ShapeDtypeStruct((n, d), F32),
            jax.ShapeDtypeStruct((n, d // 2), jnp.uint32),
            jax.ShapeDtypeStruct((n, TOP_K), jnp.int32),
            jax.ShapeDtypeStruct((n, TOP_K), F32),
        ],
        compiler_params=_params("parallel"),
    )(x2, attn2, rec2, wo_bf16, g0, b0, g1, b1, wr2, br)


def _rank_kernel(ti_ref, rank_ref, cnt_ref, run_ref, *, n_exp):
    i = pl.program_id(0)

    @pl.when(i == 0)
    def _():
        run_ref[...] = jnp.zeros_like(run_ref)

    ti = ti_ref[...]
    t = ti.shape[0]
    lane = lax.broadcasted_iota(jnp.int32, (t, n_exp), 1)
    hots = [jnp.where(lane == ti[:, k:k + 1], 1.0, 0.0) for k in range(TOP_K)]
    member = hots[0]
    for k in range(1, TOP_K):
        member = member + hots[k]
    row = lax.broadcasted_iota(jnp.int32, (t, t), 0)
    col = lax.broadcasted_iota(jnp.int32, (t, t), 1)
    strict_lower = jnp.where(row > col, 1.0, 0.0).astype(BF16)
    before = jnp.dot(strict_lower, member.astype(BF16), preferred_element_type=F32) + run_ref[...]
    ranks = [jnp.sum(hots[k] * before, axis=-1, keepdims=True) for k in range(TOP_K)]
    rank_ref[...] = jnp.concatenate(ranks, axis=1).astype(jnp.int32)
    run_ref[...] = run_ref[...] + jnp.sum(member, axis=0, keepdims=True)
    cnt_ref[...] = run_ref[...].astype(jnp.int32)


def _ranks(top_i, n_exp):
    n = top_i.shape[0]
    return pl.pallas_call(
        functools.partial(_rank_kernel, n_exp=n_exp),
        grid=(n // TOKEN_TILE,),
        in_specs=[pl.BlockSpec((TOKEN_TILE, TOP_K), lambda i: (i, 0))],
        out_specs=[pl.BlockSpec((TOKEN_TILE, TOP_K), lambda i: (i, 0)),
                   pl.BlockSpec((1, n_exp), lambda i: (0, 0))],
        out_shape=[jax.ShapeDtypeStruct((n, TOP_K), jnp.int32),
                   jax.ShapeDtypeStruct((1, n_exp), jnp.int32)],
        scratch_shapes=[pltpu.VMEM((1, n_exp), F32)],
        compiler_params=_params("arbitrary"),
    )(top_i)


def _row_copy(src_ref, src_row, dst_ref, dst_row, sem):
    return pltpu.make_async_copy(src_ref.at[pl.ds(src_row, 1), :], dst_ref.at[pl.ds(dst_row, 1), :], sem)


def _scatter_kernel(dest_ref, x_ref, init_ref, xs_ref, sem):
    del init_ref
    t = x_ref.shape[0]

    def copies(i):
        for u in range(ROW_UNROLL):
            r = i * ROW_UNROLL + u
            for k in range(TOP_K):
                yield _row_copy(x_ref, r, xs_ref, dest_ref[r * TOP_K + k], sem), (u * TOP_K + k) % 2

    def issue(i, carry):
        for cp, prio in copies(i):
            cp.start(priority=prio)
        return carry

    def drain(i, carry):
        for cp, _ in copies(i):
            cp.wait()
        return carry

    if t == ROW_UNROLL:
        issue(0, 0)
        drain(0, 0)
    else:
        lax.fori_loop(0, t // ROW_UNROLL, issue, 0)
        lax.fori_loop(0, t // ROW_UNROLL, drain, 0)


def _scatter_rows(dest_flat, packed, n_rows):
    n, w = packed.shape
    init = jnp.zeros((n_rows, w), packed.dtype)
    return pl.pallas_call(
        _scatter_kernel,
        grid=(n // ROUTE_TILE,),
        in_specs=[
            pl.BlockSpec((ROUTE_TILE * TOP_K,), lambda i: (i,), memory_space=pltpu.SMEM),
            pl.BlockSpec((ROUTE_TILE, w), lambda i: (i, 0)),
            pl.BlockSpec(memory_space=pl.ANY),
        ],
        out_specs=pl.BlockSpec(memory_space=pl.ANY),
        out_shape=jax.ShapeDtypeStruct((n_rows, w), packed.dtype),
        scratch_shapes=[pltpu.SemaphoreType.DMA(())],
        input_output_aliases={2: 0},
        compiler_params=_params("arbitrary"),
    )(dest_flat, packed, init)


def _split_up_kernel(w_ref, wg_ref, wl_ref, *, blk):
    r = lax.broadcasted_iota(jnp.int32, (2 * blk, blk), 0)
    c = lax.broadcasted_iota(jnp.int32, (2 * blk, blk), 1)
    pick_even = jnp.where(r == 2 * c, 1.0, 0.0).astype(BF16)
    pick_odd = jnp.where(r == 2 * c + 1, 1.0, 0.0).astype(BF16)
    for j in range(wg_ref.shape[2] // blk):
        w = w_ref[0, :, 2 * blk * j:2 * blk * (j + 1)].astype(BF16)
        wg_ref[0, :, blk * j:blk * (j + 1)] = jnp.dot(w, pick_even, preferred_element_type=F32).astype(BF16)
        wl_ref[0, :, blk * j:blk * (j + 1)] = jnp.dot(w, pick_odd, preferred_element_type=F32).astype(BF16)


def _split_up_weights(w_up):
    n_exp, d, de2 = w_up.shape
    de = de2 // 2
    out = jax.ShapeDtypeStruct((n_exp, d, de), BF16)
    return pl.pallas_call(
        functools.partial(_split_up_kernel, blk=256),
        grid=(n_exp,),
        in_specs=[pl.BlockSpec((1, d, de2), lambda e: (e, 0, 0))],
        out_specs=[pl.BlockSpec((1, d, de), lambda e: (e, 0, 0)),
                   pl.BlockSpec((1, d, de), lambda e: (e, 0, 0))],
        out_shape=[out, out],
        compiler_params=_params("parallel"),
    )(w_up)


def _ffn_kernel(te_ref, nv_ref, xs_ref, wg_ref, wl_ref, bg_ref, bl_ref, wd_ref, bd_ref, ys_ref):
    i = pl.program_id(0)

    @pl.when(i < nv_ref[0])
    def _():
        u = xs_ref[...]
        lo = lax.bitcast_convert_type(u << 16, F32).astype(BF16)
        hi = lax.bitcast_convert_type(u & jnp.uint32(0xFFFF0000), F32).astype(BF16)
        x = jnp.concatenate([lo, hi], axis=1)
        hg = jnp.dot(x, wg_ref[0], preferred_element_type=F32) + bg_ref[0]
        hl = jnp.dot(x, wl_ref[0], preferred_element_type=F32) + bl_ref[0]
        xg = jnp.minimum(hg, SWIGLU_LIMIT)
        xl = jnp.clip(hl, -SWIGLU_LIMIT, SWIGLU_LIMIT)
        act = xg * jax.nn.sigmoid(SWIGLU_ALPHA * xg) * (xl + 1.0)
        ys_ref[...] = jnp.dot(act.astype(BF16), wd_ref[0], preferred_element_type=F32) + bd_ref[0]

    @pl.when(i >= nv_ref[0])
    def _():
        ys_ref[...] = jnp.zeros_like(ys_ref)


def _grouped_ffn(tile_expert, n_valid, xs, wg, wl, bg, bl, wd, bd):
    n_rows, w = xs.shape
    n_exp, d, de = wg.shape
    n_tiles = n_rows // FFN_TILE
    row = lambda i, te, nv: (jnp.minimum(i, nv[0] - 1), 0)
    exp3 = lambda i, te, nv: (te[i], 0, 0)
    grid_spec = pltpu.PrefetchScalarGridSpec(
        num_scalar_prefetch=2,
        grid=(n_tiles,),
        in_specs=[
            pl.BlockSpec((FFN_TILE, w), row),
            pl.BlockSpec((1, d, de), exp3),
            pl.BlockSpec((1, d, de), exp3),
            pl.BlockSpec((1, 1, de), exp3),
            pl.BlockSpec((1, 1, de), exp3),
            pl.BlockSpec((1, de, d), exp3),
            pl.BlockSpec((1, 1, d), exp3),
        ],
        out_specs=pl.BlockSpec((FFN_TILE, d), lambda i, te, nv: (i, 0)),
    )
    return pl.pallas_call(
        _ffn_kernel,
        grid_spec=grid_spec,
        out_shape=jax.ShapeDtypeStruct((n_rows, d), F32),
        compiler_params=_params("arbitrary"),
    )(tile_expert, n_valid, xs, wg, wl, bg, bl, wd, bd)


def _combine_kernel(dest_ref, gt_ref, h1_ref, g_ref, b_ref, ys_ref, o_ref, buf_ref, sem):
    t = h1_ref.shape[0]

    def copies(i):
        for u in range(ROW_UNROLL):
            r = i * ROW_UNROLL + u
            for k in range(TOP_K):
                yield _row_copy(ys_ref, dest_ref[r * TOP_K + k], buf_ref.at[k], r, sem), (u * TOP_K + k) % 2

    def issue(i, carry):
        for cp, prio in copies(i):
            cp.start(priority=prio)
        return carry

    def drain(i, carry):
        for cp, _ in copies(i):
            cp.wait()
        return carry

    if t == ROW_UNROLL:
        issue(0, 0)
        drain(0, 0)
    else:
        lax.fori_loop(0, t // ROW_UNROLL, issue, 0)
        lax.fori_loop(0, t // ROW_UNROLL, drain, 0)

    gates = gt_ref[...]
    ffn = gates[:, 0:1] * buf_ref[0]
    for k in range(1, TOP_K):
        ffn = ffn + gates[:, k:k + 1] * buf_ref[k]
    o_ref[...] = _layer_norm(DEEPNORM_ALPHA * h1_ref[...] + ffn, g_ref[...], b_ref[...])


def _combine(dest_flat, gates, h1, g, b, ys):
    n, d = h1.shape
    return pl.pallas_call(
        _combine_kernel,
        grid=(n // ROUTE_TILE,),
        in_specs=[
            pl.BlockSpec((ROUTE_TILE * TOP_K,), lambda i: (i,), memory_space=pltpu.SMEM),
            pl.BlockSpec((ROUTE_TILE, TOP_K), lambda i: (i, 0)),
            pl.BlockSpec((ROUTE_TILE, d), lambda i: (i, 0)),
            pl.BlockSpec((1, d), lambda i: (0, 0)),
            pl.BlockSpec((1, d), lambda i: (0, 0)),
            pl.BlockSpec(memory_space=pl.ANY),
        ],
        out_specs=pl.BlockSpec((ROUTE_TILE, d), lambda i: (i, 0)),
        out_shape=jax.ShapeDtypeStruct((n, d), F32),
        scratch_shapes=[pltpu.VMEM((TOP_K, ROUTE_TILE, d), F32), pltpu.SemaphoreType.DMA(())],
        compiler_params=_params("arbitrary"),
    )(dest_flat, gates, h1, g, b, ys)


def kernel(x, emb_ln_g, emb_ln_b, w_in, attn_sink, hgrn_lb_logits, hgrn_norm_g, w_out, ln1_g, ln1_b,
           w_router, b_router, w_up, b_up, w_down, b_down, ln2_g, ln2_b):
    bsz, seq, d = x.shape
    n = bsz * seq
    n_exp = w_router.shape[-1]
    assert w_in.shape[0] == DEPTH == 1
    assert seq % ATTN_BLOCK == 0 and seq % (HGRN_CHUNK * HGRN_GROUP) == 0
    assert n % TOKEN_TILE == 0 and n % ROUTE_TILE == 0 and (ROUTE_TILE * TOP_K) % 1024 == 0
    row = lambda v: v.reshape(1, -1).astype(F32)

    x2 = x.reshape(n, d)
    g0, b0 = row(emb_ln_g), row(emb_ln_b)

    proj = _inproj(x2, g0, b0, w_in[0].astype(BF16))
    proj3 = proj.reshape(bsz, seq, -1)
    attn = _attention(proj3, attn_sink[0].astype(F32))

    lb = jnp.cumsum(jax.nn.softmax(hgrn_lb_logits.astype(F32), axis=1), axis=1)[:, 0]
    rec = _hgrn(proj3, lb, row(hgrn_norm_g[0]))

    wr = w_router[0].astype(F32)
    wr_hi = wr.astype(BF16)
    wr_lo = (wr - wr_hi.astype(F32)).astype(BF16)
    h1, packed, top_i, gates = _outproj(
        x2, attn.reshape(n, -1), rec.reshape(n, -1), w_out[0].astype(BF16),
        g0, b0, row(ln1_g[0]), row(ln1_b[0]), jnp.stack([wr_hi, wr_lo]), row(b_router[0]))

    rank, counts = _ranks(top_i, n_exp)
    counts = counts[0]
    padded = ((counts + FFN_TILE - 1) // FFN_TILE) * FFN_TILE
    ends = jnp.cumsum(padded)
    starts = ends - padded
    n_tiles = (n * TOP_K) // FFN_TILE + n_exp
    n_rows = n_tiles * FFN_TILE
    dest_flat = (starts[top_i] + rank).reshape(-1)
    tile_start = jnp.arange(n_tiles, dtype=jnp.int32) * FFN_TILE
    tile_expert = jnp.minimum(
        jnp.sum(tile_start[:, None] >= ends[None, :], axis=1), n_exp - 1).astype(jnp.int32)
    n_valid = (ends[-1:] // FFN_TILE).astype(jnp.int32)

    xs = _scatter_rows(dest_flat, packed, n_rows)

    wg, wl = _split_up_weights(w_up[0].astype(F32))
    bu = b_up[0].astype(F32)
    ys = _grouped_ffn(
        tile_expert, n_valid, xs, wg, wl,
        bu[:, None, 0::2], bu[:, None, 1::2],
        w_down[0].astype(BF16), b_down[0].astype(F32)[:, None, :])

    out = _combine(dest_flat, gates, h1, row(ln2_g[0]), row(ln2_b[0]), ys)
    return out.reshape(bsz, seq, d)
```

```python
import functools
import math

import jax
import jax.numpy as jnp
from jax import lax
from jax.experimental import pallas as pl
from jax.experimental.pallas import tpu as pltpu

F32 = jnp.float32
BF16 = jnp.bfloat16

ATTN_HEAD_DIM = 64
N_Q_HEADS = 8
N_KV_HEADS = 2
ATTN_WIDTH = N_Q_HEADS * ATTN_HEAD_DIM
KV_WIDTH = N_KV_HEADS * ATTN_HEAD_DIM
WINDOW = 128
ATTN_BLOCK = 128
HGRN_HEAD_DIM = 128
N_HGRN_HEADS = 4
HGRN_WIDTH = N_HGRN_HEADS * HGRN_HEAD_DIM
HGRN_CHUNK = 64
TOP_K = 4
SWIGLU_LIMIT = 7.0
SWIGLU_ALPHA = 1.702
LN_EPS = 1e-5
RMS_EPS = 1e-6
DEPTH = 1
DEEPNORM_ALPHA = (2.0 * DEPTH) ** 0.25

VMEM_LIMIT_BYTES = 48 * 1024 * 1024
TOKEN_TILE = 512
ROUTE_TILE = 256
FFN_TILE = 512
ROW_UNROLL = 256
HGRN_GROUP = 32


def _params(*sem):
    return pltpu.CompilerParams(dimension_semantics=sem, vmem_limit_bytes=VMEM_LIMIT_BYTES)


def _layer_norm(x, g, b):
    mu = jnp.mean(x, axis=-1, keepdims=True)
    xc = x - mu
    var = jnp.mean(xc * xc, axis=-1, keepdims=True)
    return xc * lax.rsqrt(var + LN_EPS) * g + b


def _split3(x):
    hi = x.astype(BF16)
    r = x - hi.astype(F32)
    mid = r.astype(BF16)
    lo = (r - mid.astype(F32)).astype(BF16)
    return hi, mid, lo


def _inproj_kernel(x_ref, g_ref, b_ref, w_ref, o_ref, *, col_chunk):
    h = _layer_norm(x_ref[...], g_ref[...], b_ref[...]).astype(BF16)
    n_cols = o_ref.shape[1]
    for c in range(0, n_cols, col_chunk):
        o_ref[:, c:c + col_chunk] = jnp.dot(
            h, w_ref[:, c:c + col_chunk], preferred_element_type=F32).astype(o_ref.dtype)


def _inproj(x2, g, b, w_bf16):
    n, d = x2.shape
    cols = w_bf16.shape[1]
    return pl.pallas_call(
        functools.partial(_inproj_kernel, col_chunk=256),
        grid=(n // TOKEN_TILE,),
        in_specs=[
            pl.BlockSpec((TOKEN_TILE, d), lambda i: (i, 0)),
            pl.BlockSpec((1, d), lambda i: (0, 0)),
            pl.BlockSpec((1, d), lambda i: (0, 0)),
            pl.BlockSpec((d, cols), lambda i: (0, 0)),
        ],
        out_specs=pl.BlockSpec((TOKEN_TILE, cols), lambda i: (i, 0)),
        out_shape=jax.ShapeDtypeStruct((n, cols), BF16),
        compiler_params=_params("parallel"),
    )(x2, g, b, w_bf16)


def _attn_block(q, kv, bias_ref, sink_ref):
    blk, hd = ATTN_BLOCK, ATTN_HEAD_DIM
    rep = N_Q_HEADS // N_KV_HEADS
    q = q * (1.0 / math.sqrt(hd))
    ones = jnp.ones((8, 3 * blk), BF16)
    outs = []
    for g in range(N_KV_HEADS):
        heads = range(g * rep, (g + 1) * rep)
        qg = jnp.concatenate([q[:, h * hd:(h + 1) * hd] for h in heads], axis=0)
        kg = kv[:, g * hd:(g + 1) * hd]
        vg = kv[:, KV_WIDTH + g * hd:KV_WIDTH + (g + 1) * hd]
        st = lax.dot_general(kg, qg, (((1,), (1,)), ((), ())), preferred_element_type=F32) + bias_ref[0, g]
        sink = jnp.concatenate([jnp.full((1, blk), sink_ref[h], F32) for h in heads], axis=1)
        m = jnp.maximum(jnp.max(st, axis=0, keepdims=True), sink)
        e = jnp.exp(st - m).astype(BF16)
        pv = lax.dot_general(vg, e, (((0,), (0,)), ((), ())), preferred_element_type=F32)
        den = jnp.dot(ones, e, preferred_element_type=F32)[0:1] + jnp.exp(sink - m)
        og = pv / den
        outs.extend(og[:, r * blk:(r + 1) * blk].T for r in range(rep))
    return jnp.concatenate(outs, axis=1)


def _attn_kernel(sink_ref, bias_a_ref, bias_b_ref, q_ref, k0_ref, k1_ref, k2_ref, k3_ref, o_ref):
    blk = ATTN_BLOCK
    k0, k1, k2, k3 = k0_ref[0], k1_ref[0], k2_ref[0], k3_ref[0]
    o_a = _attn_block(q_ref[0, :blk, :], jnp.concatenate([k0, k1, k2], axis=0), bias_a_ref, sink_ref)
    o_b = _attn_block(q_ref[0, blk:, :], jnp.concatenate([k1, k2, k3], axis=0), bias_b_ref, sink_ref)
    o_ref[0, :blk, :] = o_a.astype(o_ref.dtype)
    o_ref[0, blk:, :] = o_b.astype(o_ref.dtype)


def _alibi_window_bias():
    blk = ATTN_BLOCK
    rep = N_Q_HEADS // N_KV_HEADS
    krel = jnp.arange(3 * blk)[:, None] - blk
    qpos = jnp.arange(blk)[None, :]
    dist = jnp.abs(krel - qpos)
    slopes = jnp.asarray([2.0 ** (-8.0 * (h + 1) / N_Q_HEADS) for h in range(N_Q_HEADS)], F32)
    bias = jnp.where(dist <= WINDOW, -slopes[:, None, None] * dist.astype(F32), -jnp.inf)
    bias = bias.reshape(N_KV_HEADS, rep, 3 * blk, blk).transpose(0, 2, 1, 3).reshape(N_KV_HEADS, 3 * blk, rep * blk)
    no_prev = jnp.where(krel >= 0, 0.0, -jnp.inf)
    no_next = jnp.where(krel < blk, 0.0, -jnp.inf)
    return jnp.stack([bias, bias + no_prev, bias + no_next])


def _attention(proj3, sink):
    b, s, _ = proj3.shape
    nb = s // ATTN_BLOCK
    nb2 = nb // 2
    kv_col = ATTN_WIDTH // (2 * KV_WIDTH)
    kv_spec = lambda off: pl.BlockSpec(
        (1, ATTN_BLOCK, 2 * KV_WIDTH), lambda i, j: (i, jnp.clip(2 * j + off, 0, nb - 1), kv_col))
    bias = _alibi_window_bias()
    bias_spec = lambda f: pl.BlockSpec((1,) + bias.shape[1:], f)
    return pl.pallas_call(
        _attn_kernel,
        grid=(b, nb2),
        in_specs=[
            pl.BlockSpec(memory_space=pltpu.SMEM),
            bias_spec(lambda i, j: (jnp.where(j == 0, 1, 0), 0, 0, 0)),
            bias_spec(lambda i, j: (jnp.where(j == nb2 - 1, 2, 0), 0, 0, 0)),
            pl.BlockSpec((1, 2 * ATTN_BLOCK, ATTN_WIDTH), lambda i, j: (i, j, 0)),
            kv_spec(-1), kv_spec(0), kv_spec(1), kv_spec(2),
        ],
        out_specs=pl.BlockSpec((1, 2 * ATTN_BLOCK, ATTN_WIDTH), lambda i, j: (i, j, 0)),
        out_shape=jax.ShapeDtypeStruct((b, s, ATTN_WIDTH), BF16),
        compiler_params=_params("parallel", "parallel"),
    )(sink, bias, bias, proj3, proj3, proj3, proj3, proj3)


def _hgrn_group(q, v, z, lb, state_t, tri, forward):
    g, c, dk = q.shape
    f = lb + (1.0 - lb) * jax.nn.sigmoid(z)
    log_f = jnp.log(f)
    k = 1.0 - f
    tri_g = jnp.broadcast_to(tri[None], (g, c, c))
    pieces = jnp.concatenate(_split3(log_f), axis=2)
    cum3 = jnp.einsum("gts,gsd->gtd", tri_g, pieces, preferred_element_type=F32)
    cum = cum3[:, :, :dk] + cum3[:, :, dk:2 * dk] + cum3[:, :, 2 * dk:]
    if forward:
        ref = cum[:, c // 2:c // 2 + 1, :]
        last = cum[:, c - 1:c, :]
    else:
        ref = cum[:, c - 1 - c // 2:c - c // 2, :]
        last = cum[:, 0:1, :]
    q_rel = (q * jnp.exp(cum - ref)).astype(BF16)
    k_rel = (k * jnp.exp(ref - cum)).astype(BF16)
    a = jnp.einsum("gtd,gsd->gts", q_rel, k_rel, preferred_element_type=F32)
    a = jnp.where(tri_g > 0, a, 0.0).astype(BF16)
    o_intra = jnp.einsum("gts,gsv->gtv", a, v, preferred_element_type=F32)
    k_dec = (k * jnp.exp(last - cum)).astype(BF16)
    kv_t = jnp.einsum("gsv,gsd->gvd", v, k_dec, preferred_element_type=F32)
    q_dec = (q * jnp.exp(cum)).astype(BF16)
    decay = jnp.exp(last)
    o_inter = [None] * g
    for j in (range(g) if forward else range(g - 1, -1, -1)):
        o_inter[j] = lax.dot_general(q_dec[j], state_t.astype(BF16), (((1,), (1,)), ((), ())),
                                     preferred_element_type=F32)
        state_t = state_t * decay[j] + kv_t[j]
    return o_intra + jnp.stack(o_inter, axis=0), state_t


def _hgrn_kernel(q_ref, v_ref, zf_ref, zb_ref, zg_ref, lb_ref, ng_ref, o_ref, of_ref, ob_ref):
    c = HGRN_CHUNK
    g = HGRN_GROUP
    seq, dk = q_ref.shape[1], q_ref.shape[2]
    dv = v_ref.shape[2]
    n_groups = seq // (g * c)
    row = lax.broadcasted_iota(jnp.int32, (c, c), 0)
    col = lax.broadcasted_iota(jnp.int32, (c, c), 1)
    lower = jnp.where(row >= col, 1.0, 0.0).astype(BF16)
    upper = jnp.where(row <= col, 1.0, 0.0).astype(BF16)
    lb_f = lb_ref[0:1, :]
    lb_b = lb_ref[1:2, :]

    def group(ref, sl, width):
        return ref[0, sl, :].astype(F32).reshape(g, c, width)

    def body(i, carry):
        st_f, st_b = carry
        sl_f = pl.ds(pl.multiple_of(i * (g * c), g * c), g * c)
        sl_b = pl.ds(pl.multiple_of((n_groups - 1 - i) * (g * c), g * c), g * c)
        o_f, st_f = _hgrn_group(group(q_ref, sl_f, dk), v_ref[0, sl_f, :].reshape(g, c, dv),
                                group(zf_ref, sl_f, dk), lb_f, st_f, lower, True)
        o_b, st_b = _hgrn_group(group(q_ref, sl_b, dk), v_ref[0, sl_b, :].reshape(g, c, dv),
                                group(zb_ref, sl_b, dk), lb_b, st_b, upper, False)
        of_ref[sl_f, :] = o_f.reshape(g * c, dv)
        ob_ref[sl_b, :] = o_b.reshape(g * c, dv)
        return st_f, st_b

    zero = jnp.zeros((dv, dk), F32)
    lax.fori_loop(0, n_groups, body, (zero, zero))

    o = of_ref[...] + ob_ref[...]
    o = o * lax.rsqrt(jnp.mean(o * o, axis=-1, keepdims=True) + RMS_EPS)
    zg = zg_ref[0].astype(F32)
    o_ref[0] = (o * ng_ref[...] * (zg * jax.nn.sigmoid(zg))).astype(o_ref.dtype)


def _hgrn(proj3, lb, norm_g):
    b, s, _ = proj3.shape
    hd = HGRN_HEAD_DIM
    base = (ATTN_WIDTH + 2 * KV_WIDTH) // hd
    col_spec = lambda part: pl.BlockSpec((1, s, hd), lambda i, h: (i, 0, base + part * N_HGRN_HEADS + h))
    return pl.pallas_call(
        _hgrn_kernel,
        grid=(b, N_HGRN_HEADS),
        in_specs=[
            col_spec(0), col_spec(1), col_spec(2), col_spec(3), col_spec(4),
            pl.BlockSpec((2, hd), lambda i, h: (0, h)),
            pl.BlockSpec((1, hd), lambda i, h: (0, h)),
        ],
        out_specs=pl.BlockSpec((1, s, hd), lambda i, h: (i, 0, h)),
        out_shape=jax.ShapeDtypeStruct((b, s, HGRN_WIDTH), BF16),
        scratch_shapes=[pltpu.VMEM((s, hd), F32), pltpu.VMEM((s, hd), F32)],
        compiler_params=_params("parallel", "parallel"),
    )(proj3, proj3, proj3, proj3, proj3, lb, norm_g)


def _outproj_kernel(x_ref, attn_ref, rec_ref, wo_ref, g0_ref, b0_ref, g1_ref, b1_ref,
                    wr_ref, br_ref, h1_ref, pk_ref, ti_ref, gt_ref):
    half = attn_ref.shape[1]
    h0 = _layer_norm(x_ref[...], g0_ref[...], b0_ref[...])
    mix = (jnp.dot(attn_ref[...], wo_ref[:half, :], preferred_element_type=F32)
           + jnp.dot(rec_ref[...], wo_ref[half:, :], preferred_element_type=F32))
    h1 = _layer_norm(DEEPNORM_ALPHA * h0 + mix, g1_ref[...], b1_ref[...])
    h1_ref[...] = h1

    d2 = h1.shape[1] // 2
    lo_bits = lax.bitcast_convert_type(h1[:, :d2].astype(BF16).astype(F32), jnp.uint32)
    hi_bits = lax.bitcast_convert_type(h1[:, d2:].astype(BF16).astype(F32), jnp.uint32)
    pk_ref[...] = hi_bits | (lo_bits >> 16)

    h_hi = h1.astype(BF16)
    h_lo = (h1 - h_hi.astype(F32)).astype(BF16)
    w_hi = wr_ref[0]
    w_lo = wr_ref[1]
    logits = (jnp.dot(h_hi, w_hi, preferred_element_type=F32)
              + jnp.dot(h_hi, w_lo, preferred_element_type=F32)
              + jnp.dot(h_lo, w_hi, preferred_element_type=F32)) + br_ref[...]

    n_exp = logits.shape[1]
    lane = lax.broadcasted_iota(jnp.int32, logits.shape, 1)
    vals, idxs = [], []
    cur = logits
    for _ in range(TOP_K):
        m = jnp.max(cur, axis=-1, keepdims=True)
        idx = jnp.min(jnp.where(cur == m, lane, n_exp), axis=-1, keepdims=True)
        vals.append(m)
        idxs.append(idx)
        cur = jnp.where(lane == idx, -jnp.inf, cur)
    top_v = jnp.concatenate(vals, axis=1)
    e = jnp.exp(top_v - vals[0])
    gt_ref[...] = e / jnp.sum(e, axis=-1, keepdims=True)
    ti_ref[...] = jnp.concatenate(idxs, axis=1)


def _outproj(x2, attn2, rec2, wo_bf16, g0, b0, g1, b1, wr2, br):
    n, d = x2.shape
    half = attn2.shape[1]
    n_exp = br.shape[1]
    tile = lambda w: pl.BlockSpec((TOKEN_TILE, w), lambda i: (i, 0))
    vec = pl.BlockSpec((1, d), lambda i: (0, 0))
    return pl.pallas_call(
        _outproj_kernel,
        grid=(n // TOKEN_TILE,),
        in_specs=[
            tile(d), tile(half), tile(half),
            pl.BlockSpec((d, d), lambda i: (0, 0)),
            vec, vec, vec, vec,
            pl.BlockSpec((2, d, n_exp), lambda i: (0, 0, 0)),
            pl.BlockSpec((1, n_exp), lambda i: (0, 0)),
        ],
        out_specs=[tile(d), tile(d // 2), tile(TOP_K), tile(TOP_K)],
        out_shape=[
            jax.ShapeDtypeStruct((n, d), F32),
            jax.ShapeDtypeStruct((n, d // 2), jnp.uint32),
            jax.ShapeDtypeStruct((n, TOP_K), jnp.int32),
            jax.ShapeDtypeStruct((n, TOP_K), F32),
        ],
        compiler_params=_params("parallel"),
    )(x2, attn2, rec2, wo_bf16, g0, b0, g1, b1, wr2, br)


def _rank_kernel(ti_ref, rank_ref, cnt_ref, run_ref, *, n_exp):
    i = pl.program_id(0)

    @pl.when(i == 0)
    def _():
        run_ref[...] = jnp.zeros_like(run_ref)

    ti = ti_ref[...]
    t = ti.shape[0]
    lane = lax.broadcasted_iota(jnp.int32, (t, n_exp), 1)
    hots = [jnp.where(lane == ti[:, k:k + 1], 1.0, 0.0) for k in range(TOP_K)]
    member = hots[0]
    for k in range(1, TOP_K):
        member = member + hots[k]
    row = lax.broadcasted_iota(jnp.int32, (t, t), 0)
    col = lax.broadcasted_iota(jnp.int32, (t, t), 1)
    strict_lower = jnp.where(row > col, 1.0, 0.0).astype(BF16)
    before = jnp.dot(strict_lower, member.astype(BF16), preferred_element_type=F32) + run_ref[...]
    ranks = [jnp.sum(hots[k] * before, axis=-1, keepdims=True) for k in range(TOP_K)]
    rank_ref[...] = jnp.concatenate(ranks, axis=1).astype(jnp.int32)
    run_ref[...] = run_ref[...] + jnp.sum(member, axis=0, keepdims=True)
    cnt_ref[...] = run_ref[...].astype(jnp.int32)


def _ranks(top_i, n_exp):
    n = top_i.shape[0]
    return pl.pallas_call(
        functools.partial(_rank_kernel, n_exp=n_exp),
        grid=(n // TOKEN_TILE,),
        in_specs=[pl.BlockSpec((TOKEN_TILE, TOP_K), lambda i: (i, 0))],
        out_specs=[pl.BlockSpec((TOKEN_TILE, TOP_K), lambda i: (i, 0)),
                   pl.BlockSpec((1, n_exp), lambda i: (0, 0))],
        out_shape=[jax.ShapeDtypeStruct((n, TOP_K), jnp.int32),
                   jax.ShapeDtypeStruct((1, n_exp), jnp.int32)],
        scratch_shapes=[pltpu.VMEM((1, n_exp), F32)],
        compiler_params=_params("arbitrary"),
    )(top_i)


def _row_copy(src_ref, src_row, dst_ref, dst_row, sem):
    return pltpu.make_async_copy(src_ref.at[pl.ds(src_row, 1), :], dst_ref.at[pl.ds(dst_row, 1), :], sem)


def _scatter_kernel(dest_ref, x_ref, init_ref, xs_ref, sem):
    del init_ref
    t = x_ref.shape[0]

    def copies(i):
        for u in range(ROW_UNROLL):
            r = i * ROW_UNROLL + u
            for k in range(TOP_K):
                yield _row_copy(x_ref, r, xs_ref, dest_ref[r * TOP_K + k], sem), (u * TOP_K + k) % 2

    def issue(i, carry):
        for cp, prio in copies(i):
            cp.start(priority=prio)
        return carry

    def drain(i, carry):
        for cp, _ in copies(i):
            cp.wait()
        return carry

    if t == ROW_UNROLL:
        issue(0, 0)
        drain(0, 0)
    else:
        lax.fori_loop(0, t // ROW_UNROLL, issue, 0)
        lax.fori_loop(0, t // ROW_UNROLL, drain, 0)


def _scatter_rows(dest_flat, packed, n_rows):
    n, w = packed.shape
    init = jnp.zeros((n_rows, w), packed.dtype)
    return pl.pallas_call(
        _scatter_kernel,
        grid=(n // ROUTE_TILE,),
        in_specs=[
            pl.BlockSpec((ROUTE_TILE * TOP_K,), lambda i: (i,), memory_space=pltpu.SMEM),
            pl.BlockSpec((ROUTE_TILE, w), lambda i: (i, 0)),
            pl.BlockSpec(memory_space=pl.ANY),
        ],
        out_specs=pl.BlockSpec(memory_space=pl.ANY),
        out_shape=jax.ShapeDtypeStruct((n_rows, w), packed.dtype),
        scratch_shapes=[pltpu.SemaphoreType.DMA(())],
        input_output_aliases={2: 0},
        compiler_params=_params("arbitrary"),
    )(dest_flat, packed, init)


def _split_up_kernel(w_ref, wg_ref, wl_ref, *, blk):
    r = lax.broadcasted_iota(jnp.int32, (2 * blk, blk), 0)
    c = lax.broadcasted_iota(jnp.int32, (2 * blk, blk), 1)
    pick_even = jnp.where(r == 2 * c, 1.0, 0.0).astype(BF16)
    pick_odd = jnp.where(r == 2 * c + 1, 1.0, 0.0).astype(BF16)
    for j in range(wg_ref.shape[2] // blk):
        w = w_ref[0, :, 2 * blk * j:2 * blk * (j + 1)].astype(BF16)
        wg_ref[0, :, blk * j:blk * (j + 1)] = jnp.dot(w, pick_even, preferred_element_type=F32).astype(BF16)
        wl_ref[0, :, blk * j:blk * (j + 1)] = jnp.dot(w, pick_odd, preferred_element_type=F32).astype(BF16)


def _split_up_weights(w_up):
    n_exp, d, de2 = w_up.shape
    de = de2 // 2
    out = jax.ShapeDtypeStruct((n_exp, d, de), BF16)
    return pl.pallas_call(
        functools.partial(_split_up_kernel, blk=256),
        grid=(n_exp,),
        in_specs=[pl.BlockSpec((1, d, de2), lambda e: (e, 0, 0))],
        out_specs=[pl.BlockSpec((1, d, de), lambda e: (e, 0, 0)),
                   pl.BlockSpec((1, d, de), lambda e: (e, 0, 0))],
        out_shape=[out, out],
        compiler_params=_params("parallel"),
    )(w_up)


def _ffn_kernel(te_ref, nv_ref, xs_ref, wg_ref, wl_ref, bg_ref, bl_ref, wd_ref, bd_ref, ys_ref):
    i = pl.program_id(0)

    @pl.when(i < nv_ref[0])
    def _():
        u = xs_ref[...]
        lo = lax.bitcast_convert_type(u << 16, F32).astype(BF16)
        hi = lax.bitcast_convert_type(u & jnp.uint32(0xFFFF0000), F32).astype(BF16)
        x = jnp.concatenate([lo, hi], axis=1)
        hg = jnp.dot(x, wg_ref[0], preferred_element_type=F32) + bg_ref[0]
        hl = jnp.dot(x, wl_ref[0], preferred_element_type=F32) + bl_ref[0]
        xg = jnp.minimum(hg, SWIGLU_LIMIT)
        xl = jnp.clip(hl, -SWIGLU_LIMIT, SWIGLU_LIMIT)
        act = xg * jax.nn.sigmoid(SWIGLU_ALPHA * xg) * (xl + 1.0)
        ys_ref[...] = jnp.dot(act.astype(BF16), wd_ref[0].astype(BF16), preferred_element_type=F32) + bd_ref[0]

    @pl.when(i >= nv_ref[0])
    def _():
        ys_ref[...] = jnp.zeros_like(ys_ref)


def _grouped_ffn(tile_expert, n_valid, xs, wg, wl, bg, bl, wd, bd):
    n_rows, w = xs.shape
    n_exp, d, de = wg.shape
    n_tiles = n_rows // FFN_TILE
    row = lambda i, te, nv: (jnp.minimum(i, nv[0] - 1), 0)
    exp3 = lambda i, te, nv: (te[i], 0, 0)
    grid_spec = pltpu.PrefetchScalarGridSpec(
        num_scalar_prefetch=2,
        grid=(n_tiles,),
        in_specs=[
            pl.BlockSpec((FFN_TILE, w), row),
            pl.BlockSpec((1, d, de), exp3),
            pl.BlockSpec((1, d, de), exp3),
            pl.BlockSpec((1, 1, de), exp3),
            pl.BlockSpec((1, 1, de), exp3),
            pl.BlockSpec((1, de, d), exp3),
            pl.BlockSpec((1, 1, d), exp3),
        ],
        out_specs=pl.BlockSpec((FFN_TILE, d), lambda i, te, nv: (i, 0)),
    )
    return pl.pallas_call(
        _ffn_kernel,
        grid_spec=grid_spec,
        out_shape=jax.ShapeDtypeStruct((n_rows, d), F32),
        compiler_params=_params("arbitrary"),
    )(tile_expert, n_valid, xs, wg, wl, bg, bl, wd, bd)


def _combine_kernel(dest_ref, gt_ref, h1_ref, g_ref, b_ref, ys_ref, o_ref, buf_ref, sem):
    t = h1_ref.shape[0]

    def copies(i):
        for u in range(ROW_UNROLL):
            r = i * ROW_UNROLL + u
            for k in range(TOP_K):
                yield _row_copy(ys_ref, dest_ref[r * TOP_K + k], buf_ref.at[k], r, sem), (u * TOP_K + k) % 2

    def issue(i, carry):
        for cp, prio in copies(i):
            cp.start(priority=prio)
        return carry

    def drain(i, carry):
        for cp, _ in copies(i):
            cp.wait()
        return carry

    if t == ROW_UNROLL:
        issue(0, 0)
        drain(0, 0)
    else:
        lax.fori_loop(0, t // ROW_UNROLL, issue, 0)
        lax.fori_loop(0, t // ROW_UNROLL, drain, 0)

    gates = gt_ref[...]
    ffn = gates[:, 0:1] * buf_ref[0]
    for k in range(1, TOP_K):
        ffn = ffn + gates[:, k:k + 1] * buf_ref[k]
    o_ref[...] = _layer_norm(DEEPNORM_ALPHA * h1_ref[...] + ffn, g_ref[...], b_ref[...])


def _combine(dest_flat, gates, h1, g, b, ys):
    n, d = h1.shape
    return pl.pallas_call(
        _combine_kernel,
        grid=(n // ROUTE_TILE,),
        in_specs=[
            pl.BlockSpec((ROUTE_TILE * TOP_K,), lambda i: (i,), memory_space=pltpu.SMEM),
            pl.BlockSpec((ROUTE_TILE, TOP_K), lambda i: (i, 0)),
            pl.BlockSpec((ROUTE_TILE, d), lambda i: (i, 0)),
            pl.BlockSpec((1, d), lambda i: (0, 0)),
            pl.BlockSpec((1, d), lambda i: (0, 0)),
            pl.BlockSpec(memory_space=pl.ANY),
        ],
        out_specs=pl.BlockSpec((ROUTE_TILE, d), lambda i: (i, 0)),
        out_shape=jax.ShapeDtypeStruct((n, d), F32),
        scratch_shapes=[pltpu.VMEM((TOP_K, ROUTE_TILE, d), F32), pltpu.SemaphoreType.DMA(())],
        compiler_params=_params("arbitrary"),
    )(dest_flat, gates, h1, g, b, ys)


def kernel(x, emb_ln_g, emb_ln_b, w_in, attn_sink, hgrn_lb_logits, hgrn_norm_g, w_out, ln1_g, ln1_b,
           w_router, b_router, w_up, b_up, w_down, b_down, ln2_g, ln2_b):
    bsz, seq, d = x.shape
    n = bsz * seq
    n_exp = w_router.shape[-1]
    assert w_in.shape[0] == DEPTH == 1
    assert seq % (2 * ATTN_BLOCK) == 0 and seq % (HGRN_CHUNK * HGRN_GROUP) == 0
    assert n % TOKEN_TILE == 0 and n % ROUTE_TILE == 0 and (ROUTE_TILE * TOP_K) % 1024 == 0
    row = lambda v: v.reshape(1, -1).astype(F32)

    x2 = x.reshape(n, d)
    g0, b0 = row(emb_ln_g), row(emb_ln_b)

    proj = _inproj(x2, g0, b0, w_in[0].astype(BF16))
    proj3 = proj.reshape(bsz, seq, -1)
    attn = _attention(proj3, attn_sink[0].astype(F32))

    lb = jnp.cumsum(jax.nn.softmax(hgrn_lb_logits.astype(F32), axis=1), axis=1)[:, 0]
    rec = _hgrn(proj3, lb, row(hgrn_norm_g[0]))

    wr = w_router[0].astype(F32)
    wr_hi = wr.astype(BF16)
    wr_lo = (wr - wr_hi.astype(F32)).astype(BF16)
    h1, packed, top_i, gates = _outproj(
        x2, attn.reshape(n, -1), rec.reshape(n, -1), w_out[0].astype(BF16),
        g0, b0, row(ln1_g[0]), row(ln1_b[0]), jnp.stack([wr_hi, wr_lo]), row(b_router[0]))

    rank, counts = _ranks(top_i, n_exp)
    counts = counts[0]
    padded = ((counts + FFN_TILE - 1) // FFN_TILE) * FFN_TILE
    ends = jnp.cumsum(padded)
    starts = ends - padded
    n_tiles = (n * TOP_K) // FFN_TILE + n_exp
    n_rows = n_tiles * FFN_TILE
    dest_flat = (starts[top_i] + rank).reshape(-1)
    tile_start = jnp.arange(n_tiles, dtype=jnp.int32) * FFN_TILE
    tile_expert = jnp.minimum(
        jnp.sum(tile_start[:, None] >= ends[None, :], axis=1), n_exp - 1).astype(jnp.int32)
    n_valid = (ends[-1:] // FFN_TILE).astype(jnp.int32)

    xs = _scatter_rows(dest_flat, packed, n_rows)

    wg, wl = _split_up_weights(w_up[0].astype(F32))
    bu = b_up[0].astype(F32)
    ys = _grouped_ffn(
        tile_expert, n_valid, xs, wg, wl,
        bu[:, None, 0::2], bu[:, None, 1::2],
        w_down[0].astype(F32), b_down[0].astype(F32)[:, None, :])

    out = _combine(dest_flat, gates, h1, row(ln2_g[0]), row(ln2_b[0]), ys)
    return out.reshape(bsz, seq, d)
```

```python
import functools
import math

import jax
import jax.numpy as jnp
from jax import lax
from jax.experimental import pallas as pl
from jax.experimental.pallas import tpu as pltpu

F32 = jnp.float32
BF16 = jnp.bfloat16

ATTN_HEAD_DIM = 64
N_Q_HEADS = 8
N_KV_HEADS = 2
ATTN_WIDTH = N_Q_HEADS * ATTN_HEAD_DIM
KV_WIDTH = N_KV_HEADS * ATTN_HEAD_DIM
WINDOW = 128
ATTN_BLOCK = 128
HGRN_HEAD_DIM = 128
N_HGRN_HEADS = 4
HGRN_WIDTH = N_HGRN_HEADS * HGRN_HEAD_DIM
HGRN_CHUNK = 64
TOP_K = 4
SWIGLU_LIMIT = 7.0
SWIGLU_ALPHA = 1.702
LN_EPS = 1e-5
RMS_EPS = 1e-6
DEPTH = 1
DEEPNORM_ALPHA = (2.0 * DEPTH) ** 0.25

VMEM_LIMIT_BYTES = 48 * 1024 * 1024
TOKEN_TILE = 512
ROW_PARTS = 4
ROUTE_TILE = 256
FFN_TILE = 512
SPLIT_BLOCK = 256
ROW_UNROLL = 256
HGRN_GROUP = 32


def _params(*sem):
    return pltpu.CompilerParams(dimension_semantics=sem, vmem_limit_bytes=VMEM_LIMIT_BYTES)


def _layer_norm(x, g, b):
    mu = jnp.mean(x, axis=-1, keepdims=True)
    xc = x - mu
    var = jnp.mean(xc * xc, axis=-1, keepdims=True)
    return xc * lax.rsqrt(var + LN_EPS) * g + b


def _split3(x):
    hi = x.astype(BF16)
    r = x - hi.astype(F32)
    mid = r.astype(BF16)
    lo = (r - mid.astype(F32)).astype(BF16)
    return hi, mid, lo


def _inproj_kernel(x_ref, g_ref, b_ref, w_ref, o_ref, *, col_chunk):
    n_rows, n_cols = o_ref.shape
    part = n_rows // ROW_PARTS
    for r in range(0, n_rows, part):
        h = _layer_norm(x_ref[r:r + part, :], g_ref[...], b_ref[...]).astype(BF16)
        for c in range(0, n_cols, col_chunk):
            o_ref[r:r + part, c:c + col_chunk] = jnp.dot(
                h, w_ref[:, c:c + col_chunk], preferred_element_type=F32).astype(o_ref.dtype)


def _inproj(x2, g, b, w_bf16):
    n, d = x2.shape
    cols = w_bf16.shape[1]
    return pl.pallas_call(
        functools.partial(_inproj_kernel, col_chunk=256),
        grid=(n // TOKEN_TILE,),
        in_specs=[
            pl.BlockSpec((TOKEN_TILE, d), lambda i: (i, 0)),
            pl.BlockSpec((1, d), lambda i: (0, 0)),
            pl.BlockSpec((1, d), lambda i: (0, 0)),
            pl.BlockSpec((d, cols), lambda i: (0, 0)),
        ],
        out_specs=pl.BlockSpec((TOKEN_TILE, cols), lambda i: (i, 0)),
        out_shape=jax.ShapeDtypeStruct((n, cols), BF16),
        compiler_params=_params("parallel"),
    )(x2, g, b, w_bf16)


def _attn_block(q, kv, bias_ref, sink_ref):
    blk, hd = ATTN_BLOCK, ATTN_HEAD_DIM
    rep = N_Q_HEADS // N_KV_HEADS
    q = q * (1.0 / math.sqrt(hd))
    ones = jnp.ones((8, 3 * blk), BF16)
    outs = []
    for g in range(N_KV_HEADS):
        heads = range(g * rep, (g + 1) * rep)
        qg = jnp.concatenate([q[:, h * hd:(h + 1) * hd] for h in heads], axis=0)
        kg = kv[:, g * hd:(g + 1) * hd]
        vg = kv[:, KV_WIDTH + g * hd:KV_WIDTH + (g + 1) * hd]
        st = lax.dot_general(kg, qg, (((1,), (1,)), ((), ())), preferred_element_type=F32) + bias_ref[0, g]
        sink = jnp.concatenate([jnp.full((1, blk), sink_ref[h], F32) for h in heads], axis=1)
        m = jnp.maximum(jnp.max(st, axis=0, keepdims=True), sink)
        e = jnp.exp(st - m).astype(BF16)
        pv = lax.dot_general(vg, e, (((0,), (0,)), ((), ())), preferred_element_type=F32)
        den = jnp.dot(ones, e, preferred_element_type=F32)[0:1] + jnp.exp(sink - m)
        og = pv / den
        outs.extend(og[:, r * blk:(r + 1) * blk].T for r in range(rep))
    return jnp.concatenate(outs, axis=1)


def _attn_kernel(sink_ref, bias_a_ref, bias_b_ref, q_ref, k0_ref, k1_ref, k2_ref, k3_ref, o_ref):
    blk = ATTN_BLOCK
    k0, k1, k2, k3 = k0_ref[0], k1_ref[0], k2_ref[0], k3_ref[0]
    o_a = _attn_block(q_ref[0, :blk, :], jnp.concatenate([k0, k1, k2], axis=0), bias_a_ref, sink_ref)
    o_b = _attn_block(q_ref[0, blk:, :], jnp.concatenate([k1, k2, k3], axis=0), bias_b_ref, sink_ref)
    o_ref[0, :blk, :] = o_a.astype(o_ref.dtype)
    o_ref[0, blk:, :] = o_b.astype(o_ref.dtype)


def _alibi_window_bias():
    blk = ATTN_BLOCK
    rep = N_Q_HEADS // N_KV_HEADS
    krel = jnp.arange(3 * blk)[:, None] - blk
    qpos = jnp.arange(blk)[None, :]
    dist = jnp.abs(krel - qpos)
    slopes = jnp.asarray([2.0 ** (-8.0 * (h + 1) / N_Q_HEADS) for h in range(N_Q_HEADS)], F32)
    bias = jnp.where(dist <= WINDOW, -slopes[:, None, None] * dist.astype(F32), -jnp.inf)
    bias = bias.reshape(N_KV_HEADS, rep, 3 * blk, blk).transpose(0, 2, 1, 3).reshape(N_KV_HEADS, 3 * blk, rep * blk)
    no_prev = jnp.where(krel >= 0, 0.0, -jnp.inf)
    no_next = jnp.where(krel < blk, 0.0, -jnp.inf)
    return jnp.stack([bias, bias + no_prev, bias + no_next])


def _attention(proj3, sink):
    b, s, _ = proj3.shape
    nb = s // ATTN_BLOCK
    nb2 = nb // 2
    kv_col = ATTN_WIDTH // (2 * KV_WIDTH)
    kv_spec = lambda off: pl.BlockSpec(
        (1, ATTN_BLOCK, 2 * KV_WIDTH), lambda i, j: (i, jnp.clip(2 * j + off, 0, nb - 1), kv_col))
    bias = _alibi_window_bias()
    bias_spec = lambda f: pl.BlockSpec((1,) + bias.shape[1:], f)
    return pl.pallas_call(
        _attn_kernel,
        grid=(b, nb2),
        in_specs=[
            pl.BlockSpec(memory_space=pltpu.SMEM),
            bias_spec(lambda i, j: (jnp.where(j == 0, 1, 0), 0, 0, 0)),
            bias_spec(lambda i, j: (jnp.where(j == nb2 - 1, 2, 0), 0, 0, 0)),
            pl.BlockSpec((1, 2 * ATTN_BLOCK, ATTN_WIDTH), lambda i, j: (i, j, 0)),
            kv_spec(-1), kv_spec(0), kv_spec(1), kv_spec(2),
        ],
        out_specs=pl.BlockSpec((1, 2 * ATTN_BLOCK, ATTN_WIDTH), lambda i, j: (i, j, 0)),
        out_shape=jax.ShapeDtypeStruct((b, s, ATTN_WIDTH), BF16),
        compiler_params=_params("parallel", "parallel"),
    )(sink, bias, bias, proj3, proj3, proj3, proj3, proj3)


def _hgrn_group(q, v, z, lb, state_t, tri, forward):
    g, c, dk = q.shape
    f = lb + (1.0 - lb) * jax.nn.sigmoid(z)
    log_f = jnp.log(f)
    k = 1.0 - f
    tri_g = jnp.broadcast_to(tri[None], (g, c, c))
    pieces = jnp.concatenate(_split3(log_f), axis=2)
    cum3 = jnp.einsum("gts,gsd->gtd", tri_g, pieces, preferred_element_type=F32)
    cum = cum3[:, :, :dk] + cum3[:, :, dk:2 * dk] + cum3[:, :, 2 * dk:]
    if forward:
        ref = cum[:, c // 2:c // 2 + 1, :]
        last = cum[:, c - 1:c, :]
    else:
        ref = cum[:, c - 1 - c // 2:c - c // 2, :]
        last = cum[:, 0:1, :]
    q_rel = (q * jnp.exp(cum - ref)).astype(BF16)
    k_rel = (k * jnp.exp(ref - cum)).astype(BF16)
    a = jnp.einsum("gtd,gsd->gts", q_rel, k_rel, preferred_element_type=F32)
    a = jnp.where(tri_g > 0, a, 0.0).astype(BF16)
    o_intra = jnp.einsum("gts,gsv->gtv", a, v, preferred_element_type=F32)
    k_dec = (k * jnp.exp(last - cum)).astype(BF16)
    kv_t = jnp.einsum("gsv,gsd->gvd", v, k_dec, preferred_element_type=F32)
    q_dec = (q * jnp.exp(cum)).astype(BF16)
    decay = jnp.exp(last)
    o_inter = [None] * g
    for j in (range(g) if forward else range(g - 1, -1, -1)):
        o_inter[j] = lax.dot_general(q_dec[j], state_t.astype(BF16), (((1,), (1,)), ((), ())),
                                     preferred_element_type=F32)
        state_t = state_t * decay[j] + kv_t[j]
    return o_intra + jnp.stack(o_inter, axis=0), state_t


def _hgrn_kernel(q_ref, v_ref, zf_ref, zb_ref, zg_ref, lb_ref, ng_ref, o_ref, of_ref, ob_ref):
    c = HGRN_CHUNK
    g = HGRN_GROUP
    seq, dk = q_ref.shape[1], q_ref.shape[2]
    dv = v_ref.shape[2]
    n_groups = seq // (g * c)
    row = lax.broadcasted_iota(jnp.int32, (c, c), 0)
    col = lax.broadcasted_iota(jnp.int32, (c, c), 1)
    lower = jnp.where(row >= col, 1.0, 0.0).astype(BF16)
    upper = jnp.where(row <= col, 1.0, 0.0).astype(BF16)
    lb_f = lb_ref[0:1, :]
    lb_b = lb_ref[1:2, :]

    def group(ref, sl, width):
        return ref[0, sl, :].astype(F32).reshape(g, c, width)

    def body(i, carry):
        st_f, st_b = carry
        sl_f = pl.ds(pl.multiple_of(i * (g * c), g * c), g * c)
        sl_b = pl.ds(pl.multiple_of((n_groups - 1 - i) * (g * c), g * c), g * c)
        o_f, st_f = _hgrn_group(group(q_ref, sl_f, dk), v_ref[0, sl_f, :].reshape(g, c, dv),
                                group(zf_ref, sl_f, dk), lb_f, st_f, lower, True)
        o_b, st_b = _hgrn_group(group(q_ref, sl_b, dk), v_ref[0, sl_b, :].reshape(g, c, dv),
                                group(zb_ref, sl_b, dk), lb_b, st_b, upper, False)
        of_ref[sl_f, :] = o_f.reshape(g * c, dv)
        ob_ref[sl_b, :] = o_b.reshape(g * c, dv)
        return st_f, st_b

    zero = jnp.zeros((dv, dk), F32)
    lax.fori_loop(0, n_groups, body, (zero, zero))

    o = of_ref[...] + ob_ref[...]
    o = o * lax.rsqrt(jnp.mean(o * o, axis=-1, keepdims=True) + RMS_EPS)
    zg = zg_ref[0].astype(F32)
    o_ref[0] = (o * ng_ref[...] * (zg * jax.nn.sigmoid(zg))).astype(o_ref.dtype)


def _hgrn(proj3, lb, norm_g):
    b, s, _ = proj3.shape
    hd = HGRN_HEAD_DIM
    base = (ATTN_WIDTH + 2 * KV_WIDTH) // hd
    col_spec = lambda part: pl.BlockSpec((1, s, hd), lambda i, h: (i, 0, base + part * N_HGRN_HEADS + h))
    return pl.pallas_call(
        _hgrn_kernel,
        grid=(b, N_HGRN_HEADS),
        in_specs=[
            col_spec(0), col_spec(1), col_spec(2), col_spec(3), col_spec(4),
            pl.BlockSpec((2, hd), lambda i, h: (0, h)),
            pl.BlockSpec((1, hd), lambda i, h: (0, h)),
        ],
        out_specs=pl.BlockSpec((1, s, hd), lambda i, h: (i, 0, h)),
        out_shape=jax.ShapeDtypeStruct((b, s, HGRN_WIDTH), BF16),
        scratch_shapes=[pltpu.VMEM((s, hd), F32), pltpu.VMEM((s, hd), F32)],
        compiler_params=_params("parallel", "parallel"),
    )(proj3, proj3, proj3, proj3, proj3, lb, norm_g)


def _outproj_kernel(x_ref, attn_ref, rec_ref, wo_ref, g0_ref, b0_ref, g1_ref, b1_ref,
                    wr_ref, br_ref, h1_ref, pk_ref, ti_ref, gt_ref):
    half = attn_ref.shape[1]
    h0 = _layer_norm(x_ref[...], g0_ref[...], b0_ref[...])
    mix = (jnp.dot(attn_ref[...], wo_ref[:half, :], preferred_element_type=F32)
           + jnp.dot(rec_ref[...], wo_ref[half:, :], preferred_element_type=F32))
    h1 = _layer_norm(DEEPNORM_ALPHA * h0 + mix, g1_ref[...], b1_ref[...])
    h1_ref[...] = h1

    d2 = h1.shape[1] // 2
    lo_bits = lax.bitcast_convert_type(h1[:, :d2].astype(BF16).astype(F32), jnp.uint32)
    hi_bits = lax.bitcast_convert_type(h1[:, d2:].astype(BF16).astype(F32), jnp.uint32)
    pk_ref[...] = hi_bits | (lo_bits >> 16)

    h_hi = h1.astype(BF16)
    h_lo = (h1 - h_hi.astype(F32)).astype(BF16)
    w_hi = wr_ref[0]
    w_lo = wr_ref[1]
    logits = (jnp.dot(h_hi, w_hi, preferred_element_type=F32)
              + jnp.dot(h_hi, w_lo, preferred_element_type=F32)
              + jnp.dot(h_lo, w_hi, preferred_element_type=F32)) + br_ref[...]

    n_exp = logits.shape[1]
    lane = lax.broadcasted_iota(jnp.int32, logits.shape, 1)
    vals, idxs = [], []
    cur = logits
    for _ in range(TOP_K):
        m = jnp.max(cur, axis=-1, keepdims=True)
        idx = jnp.min(jnp.where(cur == m, lane, n_exp), axis=-1, keepdims=True)
        vals.append(m)
        idxs.append(idx)
        cur = jnp.where(lane == idx, -jnp.inf, cur)
    top_v = jnp.concatenate(vals, axis=1)
    e = jnp.exp(top_v - vals[0])
    gt_ref[...] = e / jnp.sum(e, axis=-1, keepdims=True)
    ti_ref[...] = jnp.concatenate(idxs, axis=1)


def _outproj(x2, attn2, rec2, wo_bf16, g0, b0, g1, b1, wr2, br):
    n, d = x2.shape
    half = attn2.shape[1]
    n_exp = br.shape[1]
    tile = lambda w: pl.BlockSpec((TOKEN_TILE, w), lambda i: (i, 0))
    vec = pl.BlockSpec((1, d), lambda i: (0, 0))
    return pl.pallas_call(
        _outproj_kernel,
        grid=(n // TOKEN_TILE,),
        in_specs=[
            tile(d), tile(half), tile(half),
            pl.BlockSpec((d, d), lambda i: (0, 0)),
            vec, vec, vec, vec,
            pl.BlockSpec((2, d, n_exp), lambda i: (0, 0, 0)),
            pl.BlockSpec((1, n_exp), lambda i: (0, 0)),
        ],
        out_specs=[tile(d), tile(d // 2), tile(TOP_K), tile(TOP_K)],
        out_shape=[
            jax.ShapeDtypeStruct((n, d), F32),
            jax.ShapeDtypeStruct((n, d // 2), jnp.uint32),
            jax.ShapeDtypeStruct((n, TOP_K), jnp.int32),
            jax.ShapeDtypeStruct((n, TOP_K), F32),
        ],
        compiler_params=_params("parallel"),
    )(x2, attn2, rec2, wo_bf16, g0, b0, g1, b1, wr2, br)


def _rank_kernel(ti_ref, rank_ref, cnt_ref, run_ref, *, n_exp):
    i = pl.program_id(0)

    @pl.when(i == 0)
    def _():
        run_ref[...] = jnp.zeros_like(run_ref)

    ti = ti_ref[...]
    t = ti.shape[0]
    lane = lax.broadcasted_iota(jnp.int32, (t, n_exp), 1)
    hots = [jnp.where(lane == ti[:, k:k + 1], 1.0, 0.0) for k in range(TOP_K)]
    member = hots[0]
    for k in range(1, TOP_K):
        member = member + hots[k]
    row = lax.broadcasted_iota(jnp.int32, (t, t), 0)
    col = lax.broadcasted_iota(jnp.int32, (t, t), 1)
    strict_lower = jnp.where(row > col, 1.0, 0.0).astype(BF16)
    before = jnp.dot(strict_lower, member.astype(BF16), preferred_element_type=F32) + run_ref[...]
    ranks = [jnp.sum(hots[k] * before, axis=-1, keepdims=True) for k in range(TOP_K)]
    rank_ref[...] = jnp.concatenate(ranks, axis=1).astype(jnp.int32)
    run_ref[...] = run_ref[...] + jnp.sum(member, axis=0, keepdims=True)
    cnt_ref[...] = run_ref[...].astype(jnp.int32)


def _ranks(top_i, n_exp):
    n = top_i.shape[0]
    return pl.pallas_call(
        functools.partial(_rank_kernel, n_exp=n_exp),
        grid=(n // TOKEN_TILE,),
        in_specs=[pl.BlockSpec((TOKEN_TILE, TOP_K), lambda i: (i, 0))],
        out_specs=[pl.BlockSpec((TOKEN_TILE, TOP_K), lambda i: (i, 0)),
                   pl.BlockSpec((1, n_exp), lambda i: (0, 0))],
        out_shape=[jax.ShapeDtypeStruct((n, TOP_K), jnp.int32),
                   jax.ShapeDtypeStruct((1, n_exp), jnp.int32)],
        scratch_shapes=[pltpu.VMEM((1, n_exp), F32)],
        compiler_params=_params("arbitrary"),
    )(top_i)


def _row_copy(src_ref, src_row, dst_ref, dst_row, sem):
    return pltpu.make_async_copy(src_ref.at[pl.ds(src_row, 1), :], dst_ref.at[pl.ds(dst_row, 1), :], sem)


def _scatter_split_kernel(dest_ref, x_ref, w_ref, init_ref, xs_ref, wg_ref, wl_ref, sem, *, blk):
    del init_ref
    t = x_ref.shape[0]

    def copies(i):
        for u in range(ROW_UNROLL):
            r = i * ROW_UNROLL + u
            for k in range(TOP_K):
                yield _row_copy(x_ref, r, xs_ref, dest_ref[r * TOP_K + k], sem), (u * TOP_K + k) % 2

    def issue(i, carry):
        for cp, prio in copies(i):
            cp.start(priority=prio)
        return carry

    def drain(i, carry):
        for cp, _ in copies(i):
            cp.wait()
        return carry

    if t == ROW_UNROLL:
        issue(0, 0)
    else:
        lax.fori_loop(0, t // ROW_UNROLL, issue, 0)
    _split_up_columns(w_ref, wg_ref, wl_ref, blk)
    if t == ROW_UNROLL:
        drain(0, 0)
    else:
        lax.fori_loop(0, t // ROW_UNROLL, drain, 0)


def _split_up_columns(w_ref, wg_ref, wl_ref, blk):
    r = lax.broadcasted_iota(jnp.int32, (2 * blk, blk), 0)
    c = lax.broadcasted_iota(jnp.int32, (2 * blk, blk), 1)
    pick_even = jnp.where(r == 2 * c, 1.0, 0.0).astype(BF16)
    pick_odd = jnp.where(r == 2 * c + 1, 1.0, 0.0).astype(BF16)
    for j in range(wg_ref.shape[2] // blk):
        w = w_ref[0, :, 2 * blk * j:2 * blk * (j + 1)].astype(BF16)
        wg_ref[0, :, blk * j:blk * (j + 1)] = jnp.dot(w, pick_even, preferred_element_type=F32).astype(BF16)
        wl_ref[0, :, blk * j:blk * (j + 1)] = jnp.dot(w, pick_odd, preferred_element_type=F32).astype(BF16)


def _scatter_rows_split_weights(dest_flat, packed, n_rows, w_up):
    n, w = packed.shape
    n_exp, d, de2 = w_up.shape
    de = de2 // 2
    steps = n // ROUTE_TILE
    slabs = steps // n_exp
    assert steps == slabs * n_exp and de % (slabs * SPLIT_BLOCK) == 0
    init = jnp.zeros((n_rows, w), packed.dtype)
    w_out = jax.ShapeDtypeStruct((n_exp, d, de), BF16)
    slab = lambda width: pl.BlockSpec((1, d, width), lambda i: (i // slabs, 0, i % slabs))
    return pl.pallas_call(
        functools.partial(_scatter_split_kernel, blk=SPLIT_BLOCK),
        grid=(steps,),
        in_specs=[
            pl.BlockSpec((ROUTE_TILE * TOP_K,), lambda i: (i,), memory_space=pltpu.SMEM),
            pl.BlockSpec((ROUTE_TILE, w), lambda i: (i, 0)),
            slab(de2 // slabs),
            pl.BlockSpec(memory_space=pl.ANY),
        ],
        out_specs=[pl.BlockSpec(memory_space=pl.ANY), slab(de // slabs), slab(de // slabs)],
        out_shape=[jax.ShapeDtypeStruct((n_rows, w), packed.dtype), w_out, w_out],
        scratch_shapes=[pltpu.SemaphoreType.DMA(())],
        input_output_aliases={3: 0},
        compiler_params=_params("arbitrary"),
    )(dest_flat, packed, w_up, init)


def _ffn_kernel(te_ref, nv_ref, xs_ref, wg_ref, wl_ref, bg_ref, bl_ref, wd_ref, bd_ref, ys_ref):
    i = pl.program_id(0)

    @pl.when(i < nv_ref[0])
    def _():
        u = xs_ref[...]
        lo = lax.bitcast_convert_type(u << 16, F32).astype(BF16)
        hi = lax.bitcast_convert_type(u & jnp.uint32(0xFFFF0000), F32).astype(BF16)
        x = jnp.concatenate([lo, hi], axis=1)
        hg = jnp.dot(x, wg_ref[0], preferred_element_type=F32) + bg_ref[0]
        hl = jnp.dot(x, wl_ref[0], preferred_element_type=F32) + bl_ref[0]
        xg = jnp.minimum(hg, SWIGLU_LIMIT)
        xl = jnp.clip(hl, -SWIGLU_LIMIT, SWIGLU_LIMIT)
        act = xg * jax.nn.sigmoid(SWIGLU_ALPHA * xg) * (xl + 1.0)
        ys_ref[...] = jnp.dot(act.astype(BF16), wd_ref[0].astype(BF16), preferred_element_type=F32) + bd_ref[0]

    @pl.when(i >= nv_ref[0])
    def _():
        ys_ref[...] = jnp.zeros_like(ys_ref)


def _grouped_ffn(tile_expert, n_valid, xs, wg, wl, bg, bl, wd, bd):
    n_rows, w = xs.shape
    n_exp, d, de = wg.shape
    n_tiles = n_rows // FFN_TILE
    row = lambda i, te, nv: (jnp.minimum(i, nv[0] - 1), 0)
    exp3 = lambda i, te, nv: (te[i], 0, 0)
    grid_spec = pltpu.PrefetchScalarGridSpec(
        num_scalar_prefetch=2,
        grid=(n_tiles,),
        in_specs=[
            pl.BlockSpec((FFN_TILE, w), row),
            pl.BlockSpec((1, d, de), exp3),
            pl.BlockSpec((1, d, de), exp3),
            pl.BlockSpec((1, 1, de), exp3),
            pl.BlockSpec((1, 1, de), exp3),
            pl.BlockSpec((1, de, d), exp3),
            pl.BlockSpec((1, 1, d), exp3),
        ],
        out_specs=pl.BlockSpec((FFN_TILE, d), lambda i, te, nv: (i, 0)),
    )
    return pl.pallas_call(
        _ffn_kernel,
        grid_spec=grid_spec,
        out_shape=jax.ShapeDtypeStruct((n_rows, d), F32),
        compiler_params=_params("arbitrary"),
    )(tile_expert, n_valid, xs, wg, wl, bg, bl, wd, bd)


def _combine_kernel(dest_ref, gt_ref, h1_ref, g_ref, b_ref, ys_ref, o_ref, buf_ref, sem):
    t = h1_ref.shape[0]

    def copies(i):
        for u in range(ROW_UNROLL):
            r = i * ROW_UNROLL + u
            for k in range(TOP_K):
                yield _row_copy(ys_ref, dest_ref[r * TOP_K + k], buf_ref.at[k], r, sem), (u * TOP_K + k) % 2

    def issue(i, carry):
        for cp, prio in copies(i):
            cp.start(priority=prio)
        return carry

    def drain(i, carry):
        for cp, _ in copies(i):
            cp.wait()
        return carry

    if t == ROW_UNROLL:
        issue(0, 0)
        drain(0, 0)
    else:
        lax.fori_loop(0, t // ROW_UNROLL, issue, 0)
        lax.fori_loop(0, t // ROW_UNROLL, drain, 0)

    gates = gt_ref[...]
    ffn = gates[:, 0:1] * buf_ref[0]
    for k in range(1, TOP_K):
        ffn = ffn + gates[:, k:k + 1] * buf_ref[k]
    o_ref[...] = _layer_norm(DEEPNORM_ALPHA * h1_ref[...] + ffn, g_ref[...], b_ref[...])


def _combine(dest_flat, gates, h1, g, b, ys):
    n, d = h1.shape
    return pl.pallas_call(
        _combine_kernel,
        grid=(n // ROUTE_TILE,),
        in_specs=[
            pl.BlockSpec((ROUTE_TILE * TOP_K,), lambda i: (i,), memory_space=pltpu.SMEM),
            pl.BlockSpec((ROUTE_TILE, TOP_K), lambda i: (i, 0)),
            pl.BlockSpec((ROUTE_TILE, d), lambda i: (i, 0)),
            pl.BlockSpec((1, d), lambda i: (0, 0)),
            pl.BlockSpec((1, d), lambda i: (0, 0)),
            pl.BlockSpec(memory_space=pl.ANY),
        ],
        out_specs=pl.BlockSpec((ROUTE_TILE, d), lambda i: (i, 0)),
        out_shape=jax.ShapeDtypeStruct((n, d), F32),
        scratch_shapes=[pltpu.VMEM((TOP_K, ROUTE_TILE, d), F32), pltpu.SemaphoreType.DMA(())],
        compiler_params=_params("arbitrary"),
    )(dest_flat, gates, h1, g, b, ys)


def kernel(x, emb_ln_g, emb_ln_b, w_in, attn_sink, hgrn_lb_logits, hgrn_norm_g, w_out, ln1_g, ln1_b,
           w_router, b_router, w_up, b_up, w_down, b_down, ln2_g, ln2_b):
    bsz, seq, d = x.shape
    n = bsz * seq
    n_exp = w_router.shape[-1]
    assert w_in.shape[0] == DEPTH == 1
    assert seq % (2 * ATTN_BLOCK) == 0 and seq % (HGRN_CHUNK * HGRN_GROUP) == 0
    assert n % TOKEN_TILE == 0 and n % ROUTE_TILE == 0 and (ROUTE_TILE * TOP_K) % 1024 == 0
    row = lambda v: v.reshape(1, -1).astype(F32)

    x2 = x.reshape(n, d)
    g0, b0 = row(emb_ln_g), row(emb_ln_b)

    proj = _inproj(x2, g0, b0, w_in[0].astype(BF16))
    proj3 = proj.reshape(bsz, seq, -1)
    attn = _attention(proj3, attn_sink[0].astype(F32))

    lb = jnp.cumsum(jax.nn.softmax(hgrn_lb_logits.astype(F32), axis=1), axis=1)[:, 0]
    rec = _hgrn(proj3, lb, row(hgrn_norm_g[0]))

    wr = w_router[0].astype(F32)
    wr_hi = wr.astype(BF16)
    wr_lo = (wr - wr_hi.astype(F32)).astype(BF16)
    h1, packed, top_i, gates = _outproj(
        x2, attn.reshape(n, -1), rec.reshape(n, -1), w_out[0].astype(BF16),
        g0, b0, row(ln1_g[0]), row(ln1_b[0]), jnp.stack([wr_hi, wr_lo]), row(b_router[0]))

    rank, counts = _ranks(top_i, n_exp)
    counts = counts[0]
    padded = ((counts + FFN_TILE - 1) // FFN_TILE) * FFN_TILE
    ends = jnp.cumsum(padded)
    starts = ends - padded
    n_tiles = (n * TOP_K) // FFN_TILE + n_exp
    n_rows = n_tiles * FFN_TILE
    dest_flat = (starts[top_i] + rank).reshape(-1)
    tile_start = jnp.arange(n_tiles, dtype=jnp.int32) * FFN_TILE
    tile_expert = jnp.minimum(
        jnp.sum(tile_start[:, None] >= ends[None, :], axis=1), n_exp - 1).astype(jnp.int32)
    n_valid = (ends[-1:] // FFN_TILE).astype(jnp.int32)

    xs, wg, wl = _scatter_rows_split_weights(dest_flat, packed, n_rows, w_up[0].astype(F32))
    bu = b_up[0].astype(F32)
    ys = _grouped_ffn(
        tile_expert, n_valid, xs, wg, wl,
        bu[:, None, 0::2], bu[:, None, 1::2],
        w_down[0].astype(F32), b_down[0].astype(F32)[:, None, :])

    out = _combine(dest_flat, gates, h1, row(ln2_g[0]), row(ln2_b[0]), ys)
    return out.reshape(bsz, seq, d)
```

```python
import functools
import math

import jax
import jax.numpy as jnp
from jax import lax
from jax.experimental import pallas as pl
from jax.experimental.pallas import tpu as pltpu

F32 = jnp.float32
BF16 = jnp.bfloat16

ATTN_HEAD_DIM = 64
N_Q_HEADS = 8
N_KV_HEADS = 2
ATTN_WIDTH = N_Q_HEADS * ATTN_HEAD_DIM
KV_WIDTH = N_KV_HEADS * ATTN_HEAD_DIM
WINDOW = 128
ATTN_BLOCK = 128
HGRN_HEAD_DIM = 128
N_HGRN_HEADS = 4
HGRN_WIDTH = N_HGRN_HEADS * HGRN_HEAD_DIM
HGRN_CHUNK = 64
TOP_K = 4
SWIGLU_LIMIT = 7.0
SWIGLU_ALPHA = 1.702
LN_EPS = 1e-5
RMS_EPS = 1e-6
DEPTH = 1
DEEPNORM_ALPHA = (2.0 * DEPTH) ** 0.25

VMEM_LIMIT_BYTES = 48 * 1024 * 1024
TOKEN_TILE = 512
ROW_PARTS = 4
ROUTE_TILE = 256
FFN_TILE = 512
SPLIT_BLOCK = 256
ROW_UNROLL = 256
COMBINE_PARTS = 8
HGRN_GROUP = 32


def _params(*sem):
    return pltpu.CompilerParams(dimension_semantics=sem, vmem_limit_bytes=VMEM_LIMIT_BYTES)


def _layer_norm(x, g, b):
    mu = jnp.mean(x, axis=-1, keepdims=True)
    xc = x - mu
    var = jnp.mean(xc * xc, axis=-1, keepdims=True)
    return xc * lax.rsqrt(var + LN_EPS) * g + b


def _split3(x):
    hi = x.astype(BF16)
    r = x - hi.astype(F32)
    mid = r.astype(BF16)
    lo = (r - mid.astype(F32)).astype(BF16)
    return hi, mid, lo


def _inproj_kernel(x_ref, g_ref, b_ref, w_ref, o_ref, *, col_chunk):
    n_rows, n_cols = o_ref.shape
    part = n_rows // ROW_PARTS
    for r in range(0, n_rows, part):
        h = _layer_norm(x_ref[r:r + part, :], g_ref[...], b_ref[...]).astype(BF16)
        for c in range(0, n_cols, col_chunk):
            o_ref[r:r + part, c:c + col_chunk] = jnp.dot(
                h, w_ref[:, c:c + col_chunk], preferred_element_type=F32).astype(o_ref.dtype)


def _inproj(x2, g, b, w_bf16):
    n, d = x2.shape
    cols = w_bf16.shape[1]
    return pl.pallas_call(
        functools.partial(_inproj_kernel, col_chunk=256),
        grid=(n // TOKEN_TILE,),
        in_specs=[
            pl.BlockSpec((TOKEN_TILE, d), lambda i: (i, 0)),
            pl.BlockSpec((1, d), lambda i: (0, 0)),
            pl.BlockSpec((1, d), lambda i: (0, 0)),
            pl.BlockSpec((d, cols), lambda i: (0, 0)),
        ],
        out_specs=pl.BlockSpec((TOKEN_TILE, cols), lambda i: (i, 0)),
        out_shape=jax.ShapeDtypeStruct((n, cols), BF16),
        compiler_params=_params("parallel"),
    )(x2, g, b, w_bf16)


def _attn_block(q, kv, bias_ref, sink_ref):
    blk, hd = ATTN_BLOCK, ATTN_HEAD_DIM
    rep = N_Q_HEADS // N_KV_HEADS
    q = q * (1.0 / math.sqrt(hd))
    ones = jnp.ones((8, 3 * blk), BF16)
    outs = []
    for g in range(N_KV_HEADS):
        heads = range(g * rep, (g + 1) * rep)
        qg = jnp.concatenate([q[:, h * hd:(h + 1) * hd] for h in heads], axis=0)
        kg = kv[:, g * hd:(g + 1) * hd]
        vg = kv[:, KV_WIDTH + g * hd:KV_WIDTH + (g + 1) * hd]
        st = lax.dot_general(kg, qg, (((1,), (1,)), ((), ())), preferred_element_type=F32) + bias_ref[0, g]
        sink = jnp.concatenate([jnp.full((1, blk), sink_ref[h], F32) for h in heads], axis=1)
        m = jnp.maximum(jnp.max(st, axis=0, keepdims=True), sink)
        e = jnp.exp(st - m).astype(BF16)
        pv = lax.dot_general(vg, e, (((0,), (0,)), ((), ())), preferred_element_type=F32)
        den = jnp.dot(ones, e, preferred_element_type=F32)[0:1] + jnp.exp(sink - m)
        og = pv / den
        outs.extend(og[:, r * blk:(r + 1) * blk].T for r in range(rep))
    return jnp.concatenate(outs, axis=1)


def _attn_kernel(sink_ref, bias_a_ref, bias_b_ref, q_ref, k0_ref, k1_ref, k2_ref, k3_ref, o_ref):
    blk = ATTN_BLOCK
    k0, k1, k2, k3 = k0_ref[0], k1_ref[0], k2_ref[0], k3_ref[0]
    o_a = _attn_block(q_ref[0, :blk, :], jnp.concatenate([k0, k1, k2], axis=0), bias_a_ref, sink_ref)
    o_b = _attn_block(q_ref[0, blk:, :], jnp.concatenate([k1, k2, k3], axis=0), bias_b_ref, sink_ref)
    o_ref[0, :blk, :] = o_a.astype(o_ref.dtype)
    o_ref[0, blk:, :] = o_b.astype(o_ref.dtype)


def _alibi_window_bias():
    blk = ATTN_BLOCK
    rep = N_Q_HEADS // N_KV_HEADS
    krel = jnp.arange(3 * blk)[:, None] - blk
    qpos = jnp.arange(blk)[None, :]
    dist = jnp.abs(krel - qpos)
    slopes = jnp.asarray([2.0 ** (-8.0 * (h + 1) / N_Q_HEADS) for h in range(N_Q_HEADS)], F32)
    bias = jnp.where(dist <= WINDOW, -slopes[:, None, None] * dist.astype(F32), -jnp.inf)
    bias = bias.reshape(N_KV_HEADS, rep, 3 * blk, blk).transpose(0, 2, 1, 3).reshape(N_KV_HEADS, 3 * blk, rep * blk)
    no_prev = jnp.where(krel >= 0, 0.0, -jnp.inf)
    no_next = jnp.where(krel < blk, 0.0, -jnp.inf)
    return jnp.stack([bias, bias + no_prev, bias + no_next])


def _attention(proj3, sink):
    b, s, _ = proj3.shape
    nb = s // ATTN_BLOCK
    nb2 = nb // 2
    kv_col = ATTN_WIDTH // (2 * KV_WIDTH)
    kv_spec = lambda off: pl.BlockSpec(
        (1, ATTN_BLOCK, 2 * KV_WIDTH), lambda i, j: (i, jnp.clip(2 * j + off, 0, nb - 1), kv_col))
    bias = _alibi_window_bias()
    bias_spec = lambda f: pl.BlockSpec((1,) + bias.shape[1:], f)
    return pl.pallas_call(
        _attn_kernel,
        grid=(b, nb2),
        in_specs=[
            pl.BlockSpec(memory_space=pltpu.SMEM),
            bias_spec(lambda i, j: (jnp.where(j == 0, 1, 0), 0, 0, 0)),
            bias_spec(lambda i, j: (jnp.where(j == nb2 - 1, 2, 0), 0, 0, 0)),
            pl.BlockSpec((1, 2 * ATTN_BLOCK, ATTN_WIDTH), lambda i, j: (i, j, 0)),
            kv_spec(-1), kv_spec(0), kv_spec(1), kv_spec(2),
        ],
        out_specs=pl.BlockSpec((1, 2 * ATTN_BLOCK, ATTN_WIDTH), lambda i, j: (i, j, 0)),
        out_shape=jax.ShapeDtypeStruct((b, s, ATTN_WIDTH), BF16),
        compiler_params=_params("parallel", "parallel"),
    )(sink, bias, bias, proj3, proj3, proj3, proj3, proj3)


def _hgrn_group(q, v, z, lb, state_t, tri, forward):
    g, c, dk = q.shape
    f = lb + (1.0 - lb) * jax.nn.sigmoid(z)
    log_f = jnp.log(f)
    k = 1.0 - f
    tri_g = jnp.broadcast_to(tri[None], (g, c, c))
    pieces = jnp.concatenate(_split3(log_f), axis=2)
    cum3 = jnp.einsum("gts,gsd->gtd", tri_g, pieces, preferred_element_type=F32)
    cum = cum3[:, :, :dk] + cum3[:, :, dk:2 * dk] + cum3[:, :, 2 * dk:]
    if forward:
        ref = cum[:, c // 2:c // 2 + 1, :]
        last = cum[:, c - 1:c, :]
    else:
        ref = cum[:, c - 1 - c // 2:c - c // 2, :]
        last = cum[:, 0:1, :]
    q_rel = (q * jnp.exp(cum - ref)).astype(BF16)
    k_rel = (k * jnp.exp(ref - cum)).astype(BF16)
    a = jnp.einsum("gtd,gsd->gts", q_rel, k_rel, preferred_element_type=F32)
    a = jnp.where(tri_g > 0, a, 0.0).astype(BF16)
    o_intra = jnp.einsum("gts,gsv->gtv", a, v, preferred_element_type=F32)
    k_dec = (k * jnp.exp(last - cum)).astype(BF16)
    kv_t = jnp.einsum("gsv,gsd->gvd", v, k_dec, preferred_element_type=F32)
    q_dec = (q * jnp.exp(cum)).astype(BF16)
    decay = jnp.exp(last)
    o_inter = [None] * g
    for j in (range(g) if forward else range(g - 1, -1, -1)):
        o_inter[j] = lax.dot_general(q_dec[j], state_t.astype(BF16), (((1,), (1,)), ((), ())),
                                     preferred_element_type=F32)
        state_t = state_t * decay[j] + kv_t[j]
    return o_intra + jnp.stack(o_inter, axis=0), state_t


def _hgrn_kernel(q_ref, v_ref, zf_ref, zb_ref, zg_ref, lb_ref, ng_ref, o_ref, of_ref, ob_ref):
    c = HGRN_CHUNK
    g = HGRN_GROUP
    seq, dk = q_ref.shape[1], q_ref.shape[2]
    dv = v_ref.shape[2]
    n_groups = seq // (g * c)
    row = lax.broadcasted_iota(jnp.int32, (c, c), 0)
    col = lax.broadcasted_iota(jnp.int32, (c, c), 1)
    lower = jnp.where(row >= col, 1.0, 0.0).astype(BF16)
    upper = jnp.where(row <= col, 1.0, 0.0).astype(BF16)
    lb_f = lb_ref[0:1, :]
    lb_b = lb_ref[1:2, :]

    def group(ref, sl, width):
        return ref[0, sl, :].astype(F32).reshape(g, c, width)

    def body(i, carry):
        st_f, st_b = carry
        sl_f = pl.ds(pl.multiple_of(i * (g * c), g * c), g * c)
        sl_b = pl.ds(pl.multiple_of((n_groups - 1 - i) * (g * c), g * c), g * c)
        o_f, st_f = _hgrn_group(group(q_ref, sl_f, dk), v_ref[0, sl_f, :].reshape(g, c, dv),
                                group(zf_ref, sl_f, dk), lb_f, st_f, lower, True)
        o_b, st_b = _hgrn_group(group(q_ref, sl_b, dk), v_ref[0, sl_b, :].reshape(g, c, dv),
                                group(zb_ref, sl_b, dk), lb_b, st_b, upper, False)
        of_ref[sl_f, :] = o_f.reshape(g * c, dv)
        ob_ref[sl_b, :] = o_b.reshape(g * c, dv)
        return st_f, st_b

    zero = jnp.zeros((dv, dk), F32)
    lax.fori_loop(0, n_groups, body, (zero, zero))

    o = of_ref[...] + ob_ref[...]
    o = o * lax.rsqrt(jnp.mean(o * o, axis=-1, keepdims=True) + RMS_EPS)
    zg = zg_ref[0].astype(F32)
    o_ref[0] = (o * ng_ref[...] * (zg * jax.nn.sigmoid(zg))).astype(o_ref.dtype)


def _hgrn(proj3, lb, norm_g):
    b, s, _ = proj3.shape
    hd = HGRN_HEAD_DIM
    base = (ATTN_WIDTH + 2 * KV_WIDTH) // hd
    col_spec = lambda part: pl.BlockSpec((1, s, hd), lambda i, h: (i, 0, base + part * N_HGRN_HEADS + h))
    return pl.pallas_call(
        _hgrn_kernel,
        grid=(b, N_HGRN_HEADS),
        in_specs=[
            col_spec(0), col_spec(1), col_spec(2), col_spec(3), col_spec(4),
            pl.BlockSpec((2, hd), lambda i, h: (0, h)),
            pl.BlockSpec((1, hd), lambda i, h: (0, h)),
        ],
        out_specs=pl.BlockSpec((1, s, hd), lambda i, h: (i, 0, h)),
        out_shape=jax.ShapeDtypeStruct((b, s, HGRN_WIDTH), BF16),
        scratch_shapes=[pltpu.VMEM((s, hd), F32), pltpu.VMEM((s, hd), F32)],
        compiler_params=_params("parallel", "parallel"),
    )(proj3, proj3, proj3, proj3, proj3, lb, norm_g)


def _outproj_kernel(x_ref, attn_ref, rec_ref, wo_ref, g0_ref, b0_ref, g1_ref, b1_ref,
                    wr_ref, br_ref, h1_ref, pk_ref, ti_ref, gt_ref):
    half = attn_ref.shape[1]
    h0 = _layer_norm(x_ref[...], g0_ref[...], b0_ref[...])
    mix = (jnp.dot(attn_ref[...], wo_ref[:half, :], preferred_element_type=F32)
           + jnp.dot(rec_ref[...], wo_ref[half:, :], preferred_element_type=F32))
    h1 = _layer_norm(DEEPNORM_ALPHA * h0 + mix, g1_ref[...], b1_ref[...])
    h1_ref[...] = h1

    d2 = h1.shape[1] // 2
    lo_bits = lax.bitcast_convert_type(h1[:, :d2].astype(BF16).astype(F32), jnp.uint32)
    hi_bits = lax.bitcast_convert_type(h1[:, d2:].astype(BF16).astype(F32), jnp.uint32)
    pk_ref[...] = hi_bits | (lo_bits >> 16)

    h_hi = h1.astype(BF16)
    h_lo = (h1 - h_hi.astype(F32)).astype(BF16)
    w_hi = wr_ref[0]
    w_lo = wr_ref[1]
    logits = (jnp.dot(h_hi, w_hi, preferred_element_type=F32)
              + jnp.dot(h_hi, w_lo, preferred_element_type=F32)
              + jnp.dot(h_lo, w_hi, preferred_element_type=F32)) + br_ref[...]

    n_exp = logits.shape[1]
    lane = lax.broadcasted_iota(jnp.int32, logits.shape, 1)
    vals, idxs = [], []
    cur = logits
    for _ in range(TOP_K):
        m = jnp.max(cur, axis=-1, keepdims=True)
        idx = jnp.min(jnp.where(cur == m, lane, n_exp), axis=-1, keepdims=True)
        vals.append(m)
        idxs.append(idx)
        cur = jnp.where(lane == idx, -jnp.inf, cur)
    top_v = jnp.concatenate(vals, axis=1)
    e = jnp.exp(top_v - vals[0])
    gt_ref[...] = e / jnp.sum(e, axis=-1, keepdims=True)
    ti_ref[...] = jnp.concatenate(idxs, axis=1)


def _outproj(x2, attn2, rec2, wo_bf16, g0, b0, g1, b1, wr2, br):
    n, d = x2.shape
    half = attn2.shape[1]
    n_exp = br.shape[1]
    tile = lambda w: pl.BlockSpec((TOKEN_TILE, w), lambda i: (i, 0))
    vec = pl.BlockSpec((1, d), lambda i: (0, 0))
    return pl.pallas_call(
        _outproj_kernel,
        grid=(n // TOKEN_TILE,),
        in_specs=[
            tile(d), tile(half), tile(half),
            pl.BlockSpec((d, d), lambda i: (0, 0)),
            vec, vec, vec, vec,
            pl.BlockSpec((2, d, n_exp), lambda i: (0, 0, 0)),
            pl.BlockSpec((1, n_exp), lambda i: (0, 0)),
        ],
        out_specs=[tile(d), tile(d // 2), tile(TOP_K), tile(TOP_K)],
        out_shape=[
            jax.ShapeDtypeStruct((n, d), F32),
            jax.ShapeDtypeStruct((n, d // 2), jnp.uint32),
            jax.ShapeDtypeStruct((n, TOP_K), jnp.int32),
            jax.ShapeDtypeStruct((n, TOP_K), F32),
        ],
        compiler_params=_params("parallel"),
    )(x2, attn2, rec2, wo_bf16, g0, b0, g1, b1, wr2, br)


def _rank_kernel(ti_ref, rank_ref, cnt_ref, run_ref, *, n_exp):
    i = pl.program_id(0)

    @pl.when(i == 0)
    def _():
        run_ref[...] = jnp.zeros_like(run_ref)

    ti = ti_ref[...]
    t = ti.shape[0]
    lane = lax.broadcasted_iota(jnp.int32, (t, n_exp), 1)
    hots = [jnp.where(lane == ti[:, k:k + 1], 1.0, 0.0) for k in range(TOP_K)]
    member = hots[0]
    for k in range(1, TOP_K):
        member = member + hots[k]
    row = lax.broadcasted_iota(jnp.int32, (t, t), 0)
    col = lax.broadcasted_iota(jnp.int32, (t, t), 1)
    strict_lower = jnp.where(row > col, 1.0, 0.0).astype(BF16)
    before = jnp.dot(strict_lower, member.astype(BF16), preferred_element_type=F32) + run_ref[...]
    ranks = [jnp.sum(hots[k] * before, axis=-1, keepdims=True) for k in range(TOP_K)]
    rank_ref[...] = jnp.concatenate(ranks, axis=1).astype(jnp.int32)
    run_ref[...] = run_ref[...] + jnp.sum(member, axis=0, keepdims=True)
    cnt_ref[...] = run_ref[...].astype(jnp.int32)


def _ranks(top_i, n_exp):
    n = top_i.shape[0]
    return pl.pallas_call(
        functools.partial(_rank_kernel, n_exp=n_exp),
        grid=(n // TOKEN_TILE,),
        in_specs=[pl.BlockSpec((TOKEN_TILE, TOP_K), lambda i: (i, 0))],
        out_specs=[pl.BlockSpec((TOKEN_TILE, TOP_K), lambda i: (i, 0)),
                   pl.BlockSpec((1, n_exp), lambda i: (0, 0))],
        out_shape=[jax.ShapeDtypeStruct((n, TOP_K), jnp.int32),
                   jax.ShapeDtypeStruct((1, n_exp), jnp.int32)],
        scratch_shapes=[pltpu.VMEM((1, n_exp), F32)],
        compiler_params=_params("arbitrary"),
    )(top_i)


def _row_copy(src_ref, src_row, dst_ref, dst_row, sem):
    return pltpu.make_async_copy(src_ref.at[pl.ds(src_row, 1), :], dst_ref.at[pl.ds(dst_row, 1), :], sem)


def _scatter_split_kernel(dest_ref, x_ref, w_ref, init_ref, xs_ref, wg_ref, wl_ref, sem, *, blk):
    del init_ref
    t = x_ref.shape[0]

    def copies(i):
        for u in range(ROW_UNROLL):
            r = i * ROW_UNROLL + u
            for k in range(TOP_K):
                yield _row_copy(x_ref, r, xs_ref, dest_ref[r * TOP_K + k], sem), (u * TOP_K + k) % 2

    def issue(i, carry):
        for cp, prio in copies(i):
            cp.start(priority=prio)
        return carry

    def drain(i, carry):
        for cp, _ in copies(i):
            cp.wait()
        return carry

    if t == ROW_UNROLL:
        issue(0, 0)
    else:
        lax.fori_loop(0, t // ROW_UNROLL, issue, 0)
    _split_up_columns(w_ref, wg_ref, wl_ref, blk)
    if t == ROW_UNROLL:
        drain(0, 0)
    else:
        lax.fori_loop(0, t // ROW_UNROLL, drain, 0)


def _split_up_columns(w_ref, wg_ref, wl_ref, blk):
    r = lax.broadcasted_iota(jnp.int32, (2 * blk, blk), 0)
    c = lax.broadcasted_iota(jnp.int32, (2 * blk, blk), 1)
    pick_even = jnp.where(r == 2 * c, 1.0, 0.0).astype(BF16)
    pick_odd = jnp.where(r == 2 * c + 1, 1.0, 0.0).astype(BF16)
    for j in range(wg_ref.shape[2] // blk):
        w = w_ref[0, :, 2 * blk * j:2 * blk * (j + 1)].astype(BF16)
        wg_ref[0, :, blk * j:blk * (j + 1)] = jnp.dot(w, pick_even, preferred_element_type=F32).astype(BF16)
        wl_ref[0, :, blk * j:blk * (j + 1)] = jnp.dot(w, pick_odd, preferred_element_type=F32).astype(BF16)


def _scatter_rows_split_weights(dest_flat, packed, n_rows, w_up):
    n, w = packed.shape
    n_exp, d, de2 = w_up.shape
    de = de2 // 2
    steps = n // ROUTE_TILE
    slabs = steps // n_exp
    assert steps == slabs * n_exp and de % (slabs * SPLIT_BLOCK) == 0
    init = jnp.zeros((n_rows, w), packed.dtype)
    w_out = jax.ShapeDtypeStruct((n_exp, d, de), BF16)
    slab = lambda width: pl.BlockSpec((1, d, width), lambda i: (i // slabs, 0, i % slabs))
    return pl.pallas_call(
        functools.partial(_scatter_split_kernel, blk=SPLIT_BLOCK),
        grid=(steps,),
        in_specs=[
            pl.BlockSpec((ROUTE_TILE * TOP_K,), lambda i: (i,), memory_space=pltpu.SMEM),
            pl.BlockSpec((ROUTE_TILE, w), lambda i: (i, 0)),
            slab(de2 // slabs),
            pl.BlockSpec(memory_space=pl.ANY),
        ],
        out_specs=[pl.BlockSpec(memory_space=pl.ANY), slab(de // slabs), slab(de // slabs)],
        out_shape=[jax.ShapeDtypeStruct((n_rows, w), packed.dtype), w_out, w_out],
        scratch_shapes=[pltpu.SemaphoreType.DMA(())],
        input_output_aliases={3: 0},
        compiler_params=_params("arbitrary"),
    )(dest_flat, packed, w_up, init)


def _ffn_kernel(te_ref, nv_ref, xs_ref, wg_ref, wl_ref, bg_ref, bl_ref, wd_ref, bd_ref, ys_ref):
    i = pl.program_id(0)

    @pl.when(i < nv_ref[0])
    def _():
        u = xs_ref[...]
        lo = lax.bitcast_convert_type(u << 16, F32).astype(BF16)
        hi = lax.bitcast_convert_type(u & jnp.uint32(0xFFFF0000), F32).astype(BF16)
        x = jnp.concatenate([lo, hi], axis=1)
        hg = jnp.dot(x, wg_ref[0], preferred_element_type=F32) + bg_ref[0]
        hl = jnp.dot(x, wl_ref[0], preferred_element_type=F32) + bl_ref[0]
        xg = jnp.minimum(hg, SWIGLU_LIMIT)
        xl = jnp.clip(hl, -SWIGLU_LIMIT, SWIGLU_LIMIT)
        act = xg * jax.nn.sigmoid(SWIGLU_ALPHA * xg) * (xl + 1.0)
        ys_ref[...] = jnp.dot(act.astype(BF16), wd_ref[0].astype(BF16), preferred_element_type=F32) + bd_ref[0]

    @pl.when(i >= nv_ref[0])
    def _():
        ys_ref[...] = jnp.zeros_like(ys_ref)


def _grouped_ffn(tile_expert, n_valid, xs, wg, wl, bg, bl, wd, bd):
    n_rows, w = xs.shape
    n_exp, d, de = wg.shape
    n_tiles = n_rows // FFN_TILE
    row = lambda i, te, nv: (jnp.minimum(i, nv[0] - 1), 0)
    exp3 = lambda i, te, nv: (te[i], 0, 0)
    grid_spec = pltpu.PrefetchScalarGridSpec(
        num_scalar_prefetch=2,
        grid=(n_tiles,),
        in_specs=[
            pl.BlockSpec((FFN_TILE, w), row),
            pl.BlockSpec((1, d, de), exp3),
            pl.BlockSpec((1, d, de), exp3),
            pl.BlockSpec((1, 1, de), exp3),
            pl.BlockSpec((1, 1, de), exp3),
            pl.BlockSpec((1, de, d), exp3),
            pl.BlockSpec((1, 1, d), exp3),
        ],
        out_specs=pl.BlockSpec((FFN_TILE, d), lambda i, te, nv: (i, 0)),
    )
    return pl.pallas_call(
        _ffn_kernel,
        grid_spec=grid_spec,
        out_shape=jax.ShapeDtypeStruct((n_rows, d), F32),
        compiler_params=_params("arbitrary"),
    )(tile_expert, n_valid, xs, wg, wl, bg, bl, wd, bd)


def _combine_kernel(dest_ref, gt_ref, h1_ref, g_ref, b_ref, ys_ref, o_ref, *scratch):
    bufs, sems = scratch[:COMBINE_PARTS], scratch[COMBINE_PARTS]
    part = h1_ref.shape[0] // COMBINE_PARTS

    def copies(p):
        for u in range(part):
            r = p * part + u
            for k in range(TOP_K):
                cp = _row_copy(ys_ref, dest_ref[r * TOP_K + k], bufs[p].at[k], u, sems.at[p])
                yield cp, (u * TOP_K + k) % 2

    def issue(p):
        for cp, prio in copies(p):
            cp.start(priority=prio)

    def drain(p):
        for cp, _ in copies(p):
            cp.wait()

    def reduce(p):
        rows = slice(p * part, (p + 1) * part)
        gates = gt_ref[rows, :]
        ffn = gates[:, 0:1] * bufs[p][0]
        for k in range(1, TOP_K):
            ffn = ffn + gates[:, k:k + 1] * bufs[p][k]
        o_ref[rows, :] = _layer_norm(DEEPNORM_ALPHA * h1_ref[rows, :] + ffn, g_ref[...], b_ref[...])

    issue(0)
    issue(1)
    for p in range(COMBINE_PARTS):
        drain(p)
        if p + 2 < COMBINE_PARTS:
            issue(p + 2)
        reduce(p)


def _combine(dest_flat, gates, h1, g, b, ys):
    n, d = h1.shape
    return pl.pallas_call(
        _combine_kernel,
        grid=(n // ROUTE_TILE,),
        in_specs=[
            pl.BlockSpec((ROUTE_TILE * TOP_K,), lambda i: (i,), memory_space=pltpu.SMEM),
            pl.BlockSpec((ROUTE_TILE, TOP_K), lambda i: (i, 0)),
            pl.BlockSpec((ROUTE_TILE, d), lambda i: (i, 0)),
            pl.BlockSpec((1, d), lambda i: (0, 0)),
            pl.BlockSpec((1, d), lambda i: (0, 0)),
            pl.BlockSpec(memory_space=pl.ANY),
        ],
        out_specs=pl.BlockSpec((ROUTE_TILE, d), lambda i: (i, 0)),
        out_shape=jax.ShapeDtypeStruct((n, d), F32),
        scratch_shapes=[pltpu.VMEM((TOP_K, ROUTE_TILE // COMBINE_PARTS, d), F32)] * COMBINE_PARTS
        + [pltpu.SemaphoreType.DMA((COMBINE_PARTS,))],
        compiler_params=_params("arbitrary"),
    )(dest_flat, gates, h1, g, b, ys)


def kernel(x, emb_ln_g, emb_ln_b, w_in, attn_sink, hgrn_lb_logits, hgrn_norm_g, w_out, ln1_g, ln1_b,
           w_router, b_router, w_up, b_up, w_down, b_down, ln2_g, ln2_b):
    bsz, seq, d = x.shape
    n = bsz * seq
    n_exp = w_router.shape[-1]
    assert w_in.shape[0] == DEPTH == 1
    assert seq % (2 * ATTN_BLOCK) == 0 and seq % (HGRN_CHUNK * HGRN_GROUP) == 0
    assert n % TOKEN_TILE == 0 and n % ROUTE_TILE == 0 and (ROUTE_TILE * TOP_K) % 1024 == 0
    row = lambda v: v.reshape(1, -1).astype(F32)

    x2 = x.reshape(n, d)
    g0, b0 = row(emb_ln_g), row(emb_ln_b)

    proj = _inproj(x2, g0, b0, w_in[0].astype(BF16))
    proj3 = proj.reshape(bsz, seq, -1)
    attn = _attention(proj3, attn_sink[0].astype(F32))

    lb = jnp.cumsum(jax.nn.softmax(hgrn_lb_logits.astype(F32), axis=1), axis=1)[:, 0]
    rec = _hgrn(proj3, lb, row(hgrn_norm_g[0]))

    wr = w_router[0].astype(F32)
    wr_hi = wr.astype(BF16)
    wr_lo = (wr - wr_hi.astype(F32)).astype(BF16)
    h1, packed, top_i, gates = _outproj(
        x2, attn.reshape(n, -1), rec.reshape(n, -1), w_out[0].astype(BF16),
        g0, b0, row(ln1_g[0]), row(ln1_b[0]), jnp.stack([wr_hi, wr_lo]), row(b_router[0]))

    rank, counts = _ranks(top_i, n_exp)
    counts = counts[0]
    padded = ((counts + FFN_TILE - 1) // FFN_TILE) * FFN_TILE
    ends = jnp.cumsum(padded)
    starts = ends - padded
    n_tiles = (n * TOP_K) // FFN_TILE + n_exp
    n_rows = n_tiles * FFN_TILE
    dest_flat = (starts[top_i] + rank).reshape(-1)
    tile_start = jnp.arange(n_tiles, dtype=jnp.int32) * FFN_TILE
    tile_expert = jnp.minimum(
        jnp.sum(tile_start[:, None] >= ends[None, :], axis=1), n_exp - 1).astype(jnp.int32)
    n_valid = (ends[-1:] // FFN_TILE).astype(jnp.int32)

    xs, wg, wl = _scatter_rows_split_weights(dest_flat, packed, n_rows, w_up[0].astype(F32))
    bu = b_up[0].astype(F32)
    ys = _grouped_ffn(
        tile_expert, n_valid, xs, wg, wl,
        bu[:, None, 0::2], bu[:, None, 1::2],
        w_down[0].astype(F32), b_down[0].astype(F32)[:, None, :])

    out = _combine(dest_flat, gates, h1, row(ln2_g[0]), row(ln2_b[0]), ys)
    return out.reshape(bsz, seq, d)
```

```python
import functools
import math

import jax
import jax.numpy as jnp
from jax import lax
from jax.experimental import pallas as pl
from jax.experimental.pallas import tpu as pltpu

F32 = jnp.float32
BF16 = jnp.bfloat16

ATTN_HEAD_DIM = 64
N_Q_HEADS = 8
N_KV_HEADS = 2
ATTN_WIDTH = N_Q_HEADS * ATTN_HEAD_DIM
KV_WIDTH = N_KV_HEADS * ATTN_HEAD_DIM
WINDOW = 128
ATTN_BLOCK = 128
HGRN_HEAD_DIM = 128
N_HGRN_HEADS = 4
HGRN_WIDTH = N_HGRN_HEADS * HGRN_HEAD_DIM
HGRN_CHUNK = 64
TOP_K = 4
SWIGLU_LIMIT = 7.0
SWIGLU_ALPHA = 1.702
LN_EPS = 1e-5
RMS_EPS = 1e-6
DEPTH = 1
DEEPNORM_ALPHA = (2.0 * DEPTH) ** 0.25

VMEM_LIMIT_BYTES = 48 * 1024 * 1024
TOKEN_TILE = 512
ROW_PARTS = 4
ROUTE_TILE = 256
FFN_TILE = 512
SPLIT_BLOCK = 256
ROW_UNROLL = 256
HGRN_GROUP = 32


def _params(*sem):
    return pltpu.CompilerParams(dimension_semantics=sem, vmem_limit_bytes=VMEM_LIMIT_BYTES)


def _layer_norm(x, g, b):
    mu = jnp.mean(x, axis=-1, keepdims=True)
    xc = x - mu
    var = jnp.mean(xc * xc, axis=-1, keepdims=True)
    return xc * lax.rsqrt(var + LN_EPS) * g + b


def _split3(x):
    hi = x.astype(BF16)
    r = x - hi.astype(F32)
    mid = r.astype(BF16)
    lo = (r - mid.astype(F32)).astype(BF16)
    return hi, mid, lo


def _pack_bf16_pairs(x):
    d2 = x.shape[1] // 2
    lo_bits = lax.bitcast_convert_type(x[:, :d2].astype(BF16).astype(F32), jnp.uint32)
    hi_bits = lax.bitcast_convert_type(x[:, d2:].astype(BF16).astype(F32), jnp.uint32)
    return hi_bits | (lo_bits >> 16)


def _unpack_bf16_pairs(u):
    lo = lax.bitcast_convert_type(u << 16, F32)
    hi = lax.bitcast_convert_type(u & jnp.uint32(0xFFFF0000), F32)
    return lo, hi


def _inproj_kernel(x_ref, g_ref, b_ref, w_ref, o_ref, *, col_chunk):
    n_rows, n_cols = o_ref.shape
    part = n_rows // ROW_PARTS
    for r in range(0, n_rows, part):
        h = _layer_norm(x_ref[r:r + part, :], g_ref[...], b_ref[...]).astype(BF16)
        for c in range(0, n_cols, col_chunk):
            o_ref[r:r + part, c:c + col_chunk] = jnp.dot(
                h, w_ref[:, c:c + col_chunk], preferred_element_type=F32).astype(o_ref.dtype)


def _inproj(x2, g, b, w_bf16):
    n, d = x2.shape
    cols = w_bf16.shape[1]
    return pl.pallas_call(
        functools.partial(_inproj_kernel, col_chunk=256),
        grid=(n // TOKEN_TILE,),
        in_specs=[
            pl.BlockSpec((TOKEN_TILE, d), lambda i: (i, 0)),
            pl.BlockSpec((1, d), lambda i: (0, 0)),
            pl.BlockSpec((1, d), lambda i: (0, 0)),
            pl.BlockSpec((d, cols), lambda i: (0, 0)),
        ],
        out_specs=pl.BlockSpec((TOKEN_TILE, cols), lambda i: (i, 0)),
        out_shape=jax.ShapeDtypeStruct((n, cols), BF16),
        compiler_params=_params("parallel"),
    )(x2, g, b, w_bf16)


def _attn_block(q, kv, bias_ref, sink_ref):
    blk, hd = ATTN_BLOCK, ATTN_HEAD_DIM
    rep = N_Q_HEADS // N_KV_HEADS
    q = q * (1.0 / math.sqrt(hd))
    ones = jnp.ones((8, 3 * blk), BF16)
    outs = []
    for g in range(N_KV_HEADS):
        heads = range(g * rep, (g + 1) * rep)
        qg = jnp.concatenate([q[:, h * hd:(h + 1) * hd] for h in heads], axis=0)
        kg = kv[:, g * hd:(g + 1) * hd]
        vg = kv[:, KV_WIDTH + g * hd:KV_WIDTH + (g + 1) * hd]
        st = lax.dot_general(kg, qg, (((1,), (1,)), ((), ())), preferred_element_type=F32) + bias_ref[0, g]
        sink = jnp.concatenate([jnp.full((1, blk), sink_ref[h], F32) for h in heads], axis=1)
        m = jnp.maximum(jnp.max(st, axis=0, keepdims=True), sink)
        e = jnp.exp(st - m).astype(BF16)
        pv = lax.dot_general(vg, e, (((0,), (0,)), ((), ())), preferred_element_type=F32)
        den = jnp.dot(ones, e, preferred_element_type=F32)[0:1] + jnp.exp(sink - m)
        og = pv / den
        outs.extend(og[:, r * blk:(r + 1) * blk].T for r in range(rep))
    return jnp.concatenate(outs, axis=1)


def _attn_kernel(sink_ref, bias_a_ref, bias_b_ref, q_ref, k0_ref, k1_ref, k2_ref, k3_ref, o_ref):
    blk = ATTN_BLOCK
    k0, k1, k2, k3 = k0_ref[0], k1_ref[0], k2_ref[0], k3_ref[0]
    o_a = _attn_block(q_ref[0, :blk, :], jnp.concatenate([k0, k1, k2], axis=0), bias_a_ref, sink_ref)
    o_b = _attn_block(q_ref[0, blk:, :], jnp.concatenate([k1, k2, k3], axis=0), bias_b_ref, sink_ref)
    o_ref[0, :blk, :] = o_a.astype(o_ref.dtype)
    o_ref[0, blk:, :] = o_b.astype(o_ref.dtype)


def _alibi_window_bias():
    blk = ATTN_BLOCK
    rep = N_Q_HEADS // N_KV_HEADS
    krel = jnp.arange(3 * blk)[:, None] - blk
    qpos = jnp.arange(blk)[None, :]
    dist = jnp.abs(krel - qpos)
    slopes = jnp.asarray([2.0 ** (-8.0 * (h + 1) / N_Q_HEADS) for h in range(N_Q_HEADS)], F32)
    bias = jnp.where(dist <= WINDOW, -slopes[:, None, None] * dist.astype(F32), -jnp.inf)
    bias = bias.reshape(N_KV_HEADS, rep, 3 * blk, blk).transpose(0, 2, 1, 3).reshape(N_KV_HEADS, 3 * blk, rep * blk)
    no_prev = jnp.where(krel >= 0, 0.0, -jnp.inf)
    no_next = jnp.where(krel < blk, 0.0, -jnp.inf)
    return jnp.stack([bias, bias + no_prev, bias + no_next])


def _attention(proj3, sink):
    b, s, _ = proj3.shape
    nb = s // ATTN_BLOCK
    nb2 = nb // 2
    kv_col = ATTN_WIDTH // (2 * KV_WIDTH)
    kv_spec = lambda off: pl.BlockSpec(
        (1, ATTN_BLOCK, 2 * KV_WIDTH), lambda i, j: (i, jnp.clip(2 * j + off, 0, nb - 1), kv_col))
    bias = _alibi_window_bias()
    bias_spec = lambda f: pl.BlockSpec((1,) + bias.shape[1:], f)
    return pl.pallas_call(
        _attn_kernel,
        grid=(b, nb2),
        in_specs=[
            pl.BlockSpec(memory_space=pltpu.SMEM),
            bias_spec(lambda i, j: (jnp.where(j == 0, 1, 0), 0, 0, 0)),
            bias_spec(lambda i, j: (jnp.where(j == nb2 - 1, 2, 0), 0, 0, 0)),
            pl.BlockSpec((1, 2 * ATTN_BLOCK, ATTN_WIDTH), lambda i, j: (i, j, 0)),
            kv_spec(-1), kv_spec(0), kv_spec(1), kv_spec(2),
        ],
        out_specs=pl.BlockSpec((1, 2 * ATTN_BLOCK, ATTN_WIDTH), lambda i, j: (i, j, 0)),
        out_shape=jax.ShapeDtypeStruct((b, s, ATTN_WIDTH), BF16),
        compiler_params=_params("parallel", "parallel"),
    )(sink, bias, bias, proj3, proj3, proj3, proj3, proj3)


def _hgrn_group(q, v, z, lb, state_t, tri, forward):
    g, c, dk = q.shape
    f = lb + (1.0 - lb) * jax.nn.sigmoid(z)
    log_f = jnp.log(f)
    k = 1.0 - f
    tri_g = jnp.broadcast_to(tri[None], (g, c, c))
    pieces = jnp.concatenate(_split3(log_f), axis=2)
    cum3 = jnp.einsum("gts,gsd->gtd", tri_g, pieces, preferred_element_type=F32)
    cum = cum3[:, :, :dk] + cum3[:, :, dk:2 * dk] + cum3[:, :, 2 * dk:]
    if forward:
        ref = cum[:, c // 2:c // 2 + 1, :]
        last = cum[:, c - 1:c, :]
    else:
        ref = cum[:, c - 1 - c // 2:c - c // 2, :]
        last = cum[:, 0:1, :]
    q_rel = (q * jnp.exp(cum - ref)).astype(BF16)
    k_rel = (k * jnp.exp(ref - cum)).astype(BF16)
    a = jnp.einsum("gtd,gsd->gts", q_rel, k_rel, preferred_element_type=F32)
    a = jnp.where(tri_g > 0, a, 0.0).astype(BF16)
    o_intra = jnp.einsum("gts,gsv->gtv", a, v, preferred_element_type=F32)
    k_dec = (k * jnp.exp(last - cum)).astype(BF16)
    kv_t = jnp.einsum("gsv,gsd->gvd", v, k_dec, preferred_element_type=F32)
    q_dec = (q * jnp.exp(cum)).astype(BF16)
    decay = jnp.exp(last)
    o_inter = [None] * g
    for j in (range(g) if forward else range(g - 1, -1, -1)):
        o_inter[j] = lax.dot_general(q_dec[j], state_t.astype(BF16), (((1,), (1,)), ((), ())),
                                     preferred_element_type=F32)
        state_t = state_t * decay[j] + kv_t[j]
    return o_intra + jnp.stack(o_inter, axis=0), state_t


def _hgrn_kernel(q_ref, v_ref, zf_ref, zb_ref, zg_ref, lb_ref, ng_ref, o_ref, of_ref, ob_ref):
    c = HGRN_CHUNK
    g = HGRN_GROUP
    seq, dk = q_ref.shape[1], q_ref.shape[2]
    dv = v_ref.shape[2]
    n_groups = seq // (g * c)
    row = lax.broadcasted_iota(jnp.int32, (c, c), 0)
    col = lax.broadcasted_iota(jnp.int32, (c, c), 1)
    lower = jnp.where(row >= col, 1.0, 0.0).astype(BF16)
    upper = jnp.where(row <= col, 1.0, 0.0).astype(BF16)
    lb_f = lb_ref[0:1, :]
    lb_b = lb_ref[1:2, :]

    def group(ref, sl, width):
        return ref[0, sl, :].astype(F32).reshape(g, c, width)

    def body(i, carry):
        st_f, st_b = carry
        sl_f = pl.ds(pl.multiple_of(i * (g * c), g * c), g * c)
        sl_b = pl.ds(pl.multiple_of((n_groups - 1 - i) * (g * c), g * c), g * c)
        o_f, st_f = _hgrn_group(group(q_ref, sl_f, dk), v_ref[0, sl_f, :].reshape(g, c, dv),
                                group(zf_ref, sl_f, dk), lb_f, st_f, lower, True)
        o_b, st_b = _hgrn_group(group(q_ref, sl_b, dk), v_ref[0, sl_b, :].reshape(g, c, dv),
                                group(zb_ref, sl_b, dk), lb_b, st_b, upper, False)
        of_ref[sl_f, :] = o_f.reshape(g * c, dv)
        ob_ref[sl_b, :] = o_b.reshape(g * c, dv)
        return st_f, st_b

    zero = jnp.zeros((dv, dk), F32)
    lax.fori_loop(0, n_groups, body, (zero, zero))

    o = of_ref[...] + ob_ref[...]
    o = o * lax.rsqrt(jnp.mean(o * o, axis=-1, keepdims=True) + RMS_EPS)
    zg = zg_ref[0].astype(F32)
    o_ref[0] = (o * ng_ref[...] * (zg * jax.nn.sigmoid(zg))).astype(o_ref.dtype)


def _hgrn(proj3, lb, norm_g):
    b, s, _ = proj3.shape
    hd = HGRN_HEAD_DIM
    base = (ATTN_WIDTH + 2 * KV_WIDTH) // hd
    col_spec = lambda part: pl.BlockSpec((1, s, hd), lambda i, h: (i, 0, base + part * N_HGRN_HEADS + h))
    return pl.pallas_call(
        _hgrn_kernel,
        grid=(b, N_HGRN_HEADS),
        in_specs=[
            col_spec(0), col_spec(1), col_spec(2), col_spec(3), col_spec(4),
            pl.BlockSpec((2, hd), lambda i, h: (0, h)),
            pl.BlockSpec((1, hd), lambda i, h: (0, h)),
        ],
        out_specs=pl.BlockSpec((1, s, hd), lambda i, h: (i, 0, h)),
        out_shape=jax.ShapeDtypeStruct((b, s, HGRN_WIDTH), BF16),
        scratch_shapes=[pltpu.VMEM((s, hd), F32), pltpu.VMEM((s, hd), F32)],
        compiler_params=_params("parallel", "parallel"),
    )(proj3, proj3, proj3, proj3, proj3, lb, norm_g)


def _outproj_kernel(x_ref, attn_ref, rec_ref, wo_ref, g0_ref, b0_ref, g1_ref, b1_ref,
                    wr_ref, br_ref, h1_ref, pk_ref, ti_ref, gt_ref):
    half = attn_ref.shape[1]
    h0 = _layer_norm(x_ref[...], g0_ref[...], b0_ref[...])
    mix = (jnp.dot(attn_ref[...], wo_ref[:half, :], preferred_element_type=F32)
           + jnp.dot(rec_ref[...], wo_ref[half:, :], preferred_element_type=F32))
    h1 = _layer_norm(DEEPNORM_ALPHA * h0 + mix, g1_ref[...], b1_ref[...])
    h1_ref[...] = h1

    pk_ref[...] = _pack_bf16_pairs(h1)

    h_hi = h1.astype(BF16)
    h_lo = (h1 - h_hi.astype(F32)).astype(BF16)
    w_hi = wr_ref[0]
    w_lo = wr_ref[1]
    logits = (jnp.dot(h_hi, w_hi, preferred_element_type=F32)
              + jnp.dot(h_hi, w_lo, preferred_element_type=F32)
              + jnp.dot(h_lo, w_hi, preferred_element_type=F32)) + br_ref[...]

    n_exp = logits.shape[1]
    lane = lax.broadcasted_iota(jnp.int32, logits.shape, 1)
    vals, idxs = [], []
    cur = logits
    for _ in range(TOP_K):
        m = jnp.max(cur, axis=-1, keepdims=True)
        idx = jnp.min(jnp.where(cur == m, lane, n_exp), axis=-1, keepdims=True)
        vals.append(m)
        idxs.append(idx)
        cur = jnp.where(lane == idx, -jnp.inf, cur)
    top_v = jnp.concatenate(vals, axis=1)
    e = jnp.exp(top_v - vals[0])
    gt_ref[...] = e / jnp.sum(e, axis=-1, keepdims=True)
    ti_ref[...] = jnp.concatenate(idxs, axis=1)


def _outproj(x2, attn2, rec2, wo_bf16, g0, b0, g1, b1, wr2, br):
    n, d = x2.shape
    half = attn2.shape[1]
    n_exp = br.shape[1]
    tile = lambda w: pl.BlockSpec((TOKEN_TILE, w), lambda i: (i, 0))
    vec = pl.BlockSpec((1, d), lambda i: (0, 0))
    return pl.pallas_call(
        _outproj_kernel,
        grid=(n // TOKEN_TILE,),
        in_specs=[
            tile(d), tile(half), tile(half),
            pl.BlockSpec((d, d), lambda i: (0, 0)),
            vec, vec, vec, vec,
            pl.BlockSpec((2, d, n_exp), lambda i: (0, 0, 0)),
            pl.BlockSpec((1, n_exp), lambda i: (0, 0)),
        ],
        out_specs=[tile(d), tile(d // 2), tile(TOP_K), tile(TOP_K)],
        out_shape=[
            jax.ShapeDtypeStruct((n, d), F32),
            jax.ShapeDtypeStruct((n, d // 2), jnp.uint32),
            jax.ShapeDtypeStruct((n, TOP_K), jnp.int32),
            jax.ShapeDtypeStruct((n, TOP_K), F32),
        ],
        compiler_params=_params("parallel"),
    )(x2, attn2, rec2, wo_bf16, g0, b0, g1, b1, wr2, br)


def _rank_kernel(ti_ref, rank_ref, cnt_ref, run_ref, *, n_exp):
    i = pl.program_id(0)

    @pl.when(i == 0)
    def _():
        run_ref[...] = jnp.zeros_like(run_ref)

    ti = ti_ref[...]
    t = ti.shape[0]
    lane = lax.broadcasted_iota(jnp.int32, (t, n_exp), 1)
    hots = [jnp.where(lane == ti[:, k:k + 1], 1.0, 0.0) for k in range(TOP_K)]
    member = hots[0]
    for k in range(1, TOP_K):
        member = member + hots[k]
    row = lax.broadcasted_iota(jnp.int32, (t, t), 0)
    col = lax.broadcasted_iota(jnp.int32, (t, t), 1)
    strict_lower = jnp.where(row > col, 1.0, 0.0).astype(BF16)
    before = jnp.dot(strict_lower, member.astype(BF16), preferred_element_type=F32) + run_ref[...]
    ranks = [jnp.sum(hots[k] * before, axis=-1, keepdims=True) for k in range(TOP_K)]
    rank_ref[...] = jnp.concatenate(ranks, axis=1).astype(jnp.int32)
    run_ref[...] = run_ref[...] + jnp.sum(member, axis=0, keepdims=True)
    cnt_ref[...] = run_ref[...].astype(jnp.int32)


def _ranks(top_i, n_exp):
    n = top_i.shape[0]
    return pl.pallas_call(
        functools.partial(_rank_kernel, n_exp=n_exp),
        grid=(n // TOKEN_TILE,),
        in_specs=[pl.BlockSpec((TOKEN_TILE, TOP_K), lambda i: (i, 0))],
        out_specs=[pl.BlockSpec((TOKEN_TILE, TOP_K), lambda i: (i, 0)),
                   pl.BlockSpec((1, n_exp), lambda i: (0, 0))],
        out_shape=[jax.ShapeDtypeStruct((n, TOP_K), jnp.int32),
                   jax.ShapeDtypeStruct((1, n_exp), jnp.int32)],
        scratch_shapes=[pltpu.VMEM((1, n_exp), F32)],
        compiler_params=_params("arbitrary"),
    )(top_i)


def _row_copy(src_ref, src_row, dst_ref, dst_row, sem):
    return pltpu.make_async_copy(src_ref.at[pl.ds(src_row, 1), :], dst_ref.at[pl.ds(dst_row, 1), :], sem)


def _scatter_split_kernel(dest_ref, x_ref, w_ref, init_ref, xs_ref, wg_ref, wl_ref, sem, *, blk):
    del init_ref
    t = x_ref.shape[0]

    def copies(i):
        for u in range(ROW_UNROLL):
            r = i * ROW_UNROLL + u
            for k in range(TOP_K):
                yield _row_copy(x_ref, r, xs_ref, dest_ref[r * TOP_K + k], sem), (u * TOP_K + k) % 2

    def issue(i, carry):
        for cp, prio in copies(i):
            cp.start(priority=prio)
        return carry

    def drain(i, carry):
        for cp, _ in copies(i):
            cp.wait()
        return carry

    if t == ROW_UNROLL:
        issue(0, 0)
    else:
        lax.fori_loop(0, t // ROW_UNROLL, issue, 0)
    _split_up_columns(w_ref, wg_ref, wl_ref, blk)
    if t == ROW_UNROLL:
        drain(0, 0)
    else:
        lax.fori_loop(0, t // ROW_UNROLL, drain, 0)


def _split_up_columns(w_ref, wg_ref, wl_ref, blk):
    r = lax.broadcasted_iota(jnp.int32, (2 * blk, blk), 0)
    c = lax.broadcasted_iota(jnp.int32, (2 * blk, blk), 1)
    pick_even = jnp.where(r == 2 * c, 1.0, 0.0).astype(BF16)
    pick_odd = jnp.where(r == 2 * c + 1, 1.0, 0.0).astype(BF16)
    for j in range(wg_ref.shape[2] // blk):
        w = w_ref[0, :, 2 * blk * j:2 * blk * (j + 1)].astype(BF16)
        wg_ref[0, :, blk * j:blk * (j + 1)] = jnp.dot(w, pick_even, preferred_element_type=F32).astype(BF16)
        wl_ref[0, :, blk * j:blk * (j + 1)] = jnp.dot(w, pick_odd, preferred_element_type=F32).astype(BF16)


def _scatter_rows_split_weights(dest_flat, packed, n_rows, w_up):
    n, w = packed.shape
    n_exp, d, de2 = w_up.shape
    de = de2 // 2
    steps = n // ROUTE_TILE
    slabs = steps // n_exp
    assert steps == slabs * n_exp and de % (slabs * SPLIT_BLOCK) == 0
    init = jnp.zeros((n_rows, w), packed.dtype)
    w_out = jax.ShapeDtypeStruct((n_exp, d, de), BF16)
    slab = lambda width: pl.BlockSpec((1, d, width), lambda i: (i // slabs, 0, i % slabs))
    return pl.pallas_call(
        functools.partial(_scatter_split_kernel, blk=SPLIT_BLOCK),
        grid=(steps,),
        in_specs=[
            pl.BlockSpec((ROUTE_TILE * TOP_K,), lambda i: (i,), memory_space=pltpu.SMEM),
            pl.BlockSpec((ROUTE_TILE, w), lambda i: (i, 0)),
            slab(de2 // slabs),
            pl.BlockSpec(memory_space=pl.ANY),
        ],
        out_specs=[pl.BlockSpec(memory_space=pl.ANY), slab(de // slabs), slab(de // slabs)],
        out_shape=[jax.ShapeDtypeStruct((n_rows, w), packed.dtype), w_out, w_out],
        scratch_shapes=[pltpu.SemaphoreType.DMA(())],
        input_output_aliases={3: 0},
        compiler_params=_params("arbitrary"),
    )(dest_flat, packed, w_up, init)


def _ffn_kernel(te_ref, nv_ref, xs_ref, wg_ref, wl_ref, bg_ref, bl_ref, wd_ref, bd_ref, ys_ref):
    i = pl.program_id(0)

    @pl.when(i < nv_ref[0])
    def _():
        lo, hi = _unpack_bf16_pairs(xs_ref[...])
        x = jnp.concatenate([lo.astype(BF16), hi.astype(BF16)], axis=1)
        hg = jnp.dot(x, wg_ref[0], preferred_element_type=F32) + bg_ref[0]
        hl = jnp.dot(x, wl_ref[0], preferred_element_type=F32) + bl_ref[0]
        xg = jnp.minimum(hg, SWIGLU_LIMIT)
        xl = jnp.clip(hl, -SWIGLU_LIMIT, SWIGLU_LIMIT)
        act = xg * jax.nn.sigmoid(SWIGLU_ALPHA * xg) * (xl + 1.0)
        y = jnp.dot(act.astype(BF16), wd_ref[0].astype(BF16), preferred_element_type=F32) + bd_ref[0]
        ys_ref[...] = _pack_bf16_pairs(y)

    @pl.when(i >= nv_ref[0])
    def _():
        ys_ref[...] = jnp.zeros_like(ys_ref)


def _grouped_ffn(tile_expert, n_valid, xs, wg, wl, bg, bl, wd, bd):
    n_rows, w = xs.shape
    n_exp, d, de = wg.shape
    n_tiles = n_rows // FFN_TILE
    row = lambda i, te, nv: (jnp.minimum(i, nv[0] - 1), 0)
    exp3 = lambda i, te, nv: (te[i], 0, 0)
    grid_spec = pltpu.PrefetchScalarGridSpec(
        num_scalar_prefetch=2,
        grid=(n_tiles,),
        in_specs=[
            pl.BlockSpec((FFN_TILE, w), row),
            pl.BlockSpec((1, d, de), exp3),
            pl.BlockSpec((1, d, de), exp3),
            pl.BlockSpec((1, 1, de), exp3),
            pl.BlockSpec((1, 1, de), exp3),
            pl.BlockSpec((1, de, d), exp3),
            pl.BlockSpec((1, 1, d), exp3),
        ],
        out_specs=pl.BlockSpec((FFN_TILE, d // 2), lambda i, te, nv: (i, 0)),
    )
    return pl.pallas_call(
        _ffn_kernel,
        grid_spec=grid_spec,
        out_shape=jax.ShapeDtypeStruct((n_rows, d // 2), jnp.uint32),
        compiler_params=_params("arbitrary"),
    )(tile_expert, n_valid, xs, wg, wl, bg, bl, wd, bd)


def _combine_kernel(dest_ref, gt_ref, h1_ref, g_ref, b_ref, ys_ref, o_ref, buf_ref, sem):
    t = h1_ref.shape[0]

    def copies(i):
        for u in range(ROW_UNROLL):
            r = i * ROW_UNROLL + u
            for k in range(TOP_K):
                yield _row_copy(ys_ref, dest_ref[r * TOP_K + k], buf_ref.at[k], r, sem), (u * TOP_K + k) % 2

    def issue(i, carry):
        for cp, prio in copies(i):
            cp.start(priority=prio)
        return carry

    def drain(i, carry):
        for cp, _ in copies(i):
            cp.wait()
        return carry

    if t == ROW_UNROLL:
        issue(0, 0)
        drain(0, 0)
    else:
        lax.fori_loop(0, t // ROW_UNROLL, issue, 0)
        lax.fori_loop(0, t // ROW_UNROLL, drain, 0)

    gates = gt_ref[...]
    lo, hi = _unpack_bf16_pairs(buf_ref[0])
    ffn_lo, ffn_hi = gates[:, 0:1] * lo, gates[:, 0:1] * hi
    for k in range(1, TOP_K):
        lo, hi = _unpack_bf16_pairs(buf_ref[k])
        ffn_lo, ffn_hi = ffn_lo + gates[:, k:k + 1] * lo, ffn_hi + gates[:, k:k + 1] * hi
    ffn = jnp.concatenate([ffn_lo, ffn_hi], axis=1)
    o_ref[...] = _layer_norm(DEEPNORM_ALPHA * h1_ref[...] + ffn, g_ref[...], b_ref[...])


def _combine(dest_flat, gates, h1, g, b, ys):
    n, d = h1.shape
    return pl.pallas_call(
        _combine_kernel,
        grid=(n // ROUTE_TILE,),
        in_specs=[
            pl.BlockSpec((ROUTE_TILE * TOP_K,), lambda i: (i,), memory_space=pltpu.SMEM),
            pl.BlockSpec((ROUTE_TILE, TOP_K), lambda i: (i, 0)),
            pl.BlockSpec((ROUTE_TILE, d), lambda i: (i, 0)),
            pl.BlockSpec((1, d), lambda i: (0, 0)),
            pl.BlockSpec((1, d), lambda i: (0, 0)),
            pl.BlockSpec(memory_space=pl.ANY),
        ],
        out_specs=pl.BlockSpec((ROUTE_TILE, d), lambda i: (i, 0)),
        out_shape=jax.ShapeDtypeStruct((n, d), F32),
        scratch_shapes=[pltpu.VMEM((TOP_K, ROUTE_TILE, d // 2), jnp.uint32), pltpu.SemaphoreType.DMA(())],
        compiler_params=_params("arbitrary"),
    )(dest_flat, gates, h1, g, b, ys)


def kernel(x, emb_ln_g, emb_ln_b, w_in, attn_sink, hgrn_lb_logits, hgrn_norm_g, w_out, ln1_g, ln1_b,
           w_router, b_router, w_up, b_up, w_down, b_down, ln2_g, ln2_b):
    bsz, seq, d = x.shape
    n = bsz * seq
    n_exp = w_router.shape[-1]
    assert w_in.shape[0] == DEPTH == 1
    assert seq % (2 * ATTN_BLOCK) == 0 and seq % (HGRN_CHUNK * HGRN_GROUP) == 0
    assert n % TOKEN_TILE == 0 and n % ROUTE_TILE == 0 and (ROUTE_TILE * TOP_K) % 1024 == 0
    row = lambda v: v.reshape(1, -1).astype(F32)

    x2 = x.reshape(n, d)
    g0, b0 = row(emb_ln_g), row(emb_ln_b)

    proj = _inproj(x2, g0, b0, w_in[0].astype(BF16))
    proj3 = proj.reshape(bsz, seq, -1)
    attn = _attention(proj3, attn_sink[0].astype(F32))

    lb = jnp.cumsum(jax.nn.softmax(hgrn_lb_logits.astype(F32), axis=1), axis=1)[:, 0]
    rec = _hgrn(proj3, lb, row(hgrn_norm_g[0]))

    wr = w_router[0].astype(F32)
    wr_hi = wr.astype(BF16)
    wr_lo = (wr - wr_hi.astype(F32)).astype(BF16)
    h1, packed, top_i, gates = _outproj(
        x2, attn.reshape(n, -1), rec.reshape(n, -1), w_out[0].astype(BF16),
        g0, b0, row(ln1_g[0]), row(ln1_b[0]), jnp.stack([wr_hi, wr_lo]), row(b_router[0]))

    rank, counts = _ranks(top_i, n_exp)
    counts = counts[0]
    padded = ((counts + FFN_TILE - 1) // FFN_TILE) * FFN_TILE
    ends = jnp.cumsum(padded)
    starts = ends - padded
    n_tiles = (n * TOP_K) // FFN_TILE + n_exp
    n_rows = n_tiles * FFN_TILE
    dest_flat = (starts[top_i] + rank).reshape(-1)
    tile_start = jnp.arange(n_tiles, dtype=jnp.int32) * FFN_TILE
    tile_expert = jnp.minimum(
        jnp.sum(tile_start[:, None] >= ends[None, :], axis=1), n_exp - 1).astype(jnp.int32)
    n_valid = (ends[-1:] // FFN_TILE).astype(jnp.int32)

    xs, wg, wl = _scatter_rows_split_weights(dest_flat, packed, n_rows, w_up[0].astype(F32))
    bu = b_up[0].astype(F32)
    ys = _grouped_ffn(
        tile_expert, n_valid, xs, wg, wl,
        bu[:, None, 0::2], bu[:, None, 1::2],
        w_down[0].astype(F32), b_down[0].astype(F32)[:, None, :])

    out = _combine(dest_flat, gates, h1, row(ln2_g[0]), row(ln2_b[0]), ys)
    return out.reshape(bsz, seq, d)
```

```python
import functools
import math

import jax
import jax.numpy as jnp
from jax import lax
from jax.experimental import pallas as pl
from jax.experimental.pallas import tpu as pltpu

F32 = jnp.float32
BF16 = jnp.bfloat16

ATTN_HEAD_DIM = 64
N_Q_HEADS = 8
N_KV_HEADS = 2
ATTN_WIDTH = N_Q_HEADS * ATTN_HEAD_DIM
KV_WIDTH = N_KV_HEADS * ATTN_HEAD_DIM
WINDOW = 128
ATTN_BLOCK = 128
HGRN_HEAD_DIM = 128
N_HGRN_HEADS = 4
HGRN_WIDTH = N_HGRN_HEADS * HGRN_HEAD_DIM
HGRN_CHUNK = 64
TOP_K = 4
SWIGLU_LIMIT = 7.0
SWIGLU_ALPHA = 1.702
LN_EPS = 1e-5
RMS_EPS = 1e-6
DEPTH = 1
DEEPNORM_ALPHA = (2.0 * DEPTH) ** 0.25

VMEM_LIMIT_BYTES = 48 * 1024 * 1024
TOKEN_TILE = 512
ROW_PARTS = 4
ROUTE_TILE = 256
FFN_TILE = 512
SPLIT_BLOCK = 256
ROW_UNROLL = 256
HGRN_GROUP = 32


def _params(*sem):
    return pltpu.CompilerParams(dimension_semantics=sem, vmem_limit_bytes=VMEM_LIMIT_BYTES)


def _layer_norm(x, g, b):
    mu = jnp.mean(x, axis=-1, keepdims=True)
    xc = x - mu
    var = jnp.mean(xc * xc, axis=-1, keepdims=True)
    return xc * lax.rsqrt(var + LN_EPS) * g + b


def _split3(x):
    hi = x.astype(BF16)
    r = x - hi.astype(F32)
    mid = r.astype(BF16)
    lo = (r - mid.astype(F32)).astype(BF16)
    return hi, mid, lo


def _pack_bf16_pairs(x):
    d2 = x.shape[1] // 2
    lo_bits = lax.bitcast_convert_type(x[:, :d2].astype(BF16).astype(F32), jnp.uint32)
    hi_bits = lax.bitcast_convert_type(x[:, d2:].astype(BF16).astype(F32), jnp.uint32)
    return hi_bits | (lo_bits >> 16)


def _unpack_bf16_pairs(u):
    lo = lax.bitcast_convert_type(u << 16, F32)
    hi = lax.bitcast_convert_type(u & jnp.uint32(0xFFFF0000), F32)
    return lo, hi


def _inproj_kernel(x_ref, g_ref, b_ref, w_ref, o_ref, *, col_chunk):
    n_rows, n_cols = o_ref.shape
    part = n_rows // ROW_PARTS
    for r in range(0, n_rows, part):
        h = _layer_norm(x_ref[r:r + part, :], g_ref[...], b_ref[...]).astype(BF16)
        for c in range(0, n_cols, col_chunk):
            o_ref[r:r + part, c:c + col_chunk] = jnp.dot(
                h, w_ref[:, c:c + col_chunk], preferred_element_type=F32).astype(o_ref.dtype)


def _inproj(x2, g, b, w_bf16):
    n, d = x2.shape
    cols = w_bf16.shape[1]
    return pl.pallas_call(
        functools.partial(_inproj_kernel, col_chunk=256),
        grid=(n // TOKEN_TILE,),
        in_specs=[
            pl.BlockSpec((TOKEN_TILE, d), lambda i: (i, 0)),
            pl.BlockSpec((1, d), lambda i: (0, 0)),
            pl.BlockSpec((1, d), lambda i: (0, 0)),
            pl.BlockSpec((d, cols), lambda i: (0, 0)),
        ],
        out_specs=pl.BlockSpec((TOKEN_TILE, cols), lambda i: (i, 0)),
        out_shape=jax.ShapeDtypeStruct((n, cols), BF16),
        compiler_params=_params("parallel"),
    )(x2, g, b, w_bf16)


def _attn_block(q, kv, bias_ref, sink_ref):
    blk, hd = ATTN_BLOCK, ATTN_HEAD_DIM
    rep = N_Q_HEADS // N_KV_HEADS
    q = q * (1.0 / math.sqrt(hd))
    ones = jnp.ones((8, 3 * blk), BF16)
    outs = []
    for g in range(N_KV_HEADS):
        heads = range(g * rep, (g + 1) * rep)
        qg = jnp.concatenate([q[:, h * hd:(h + 1) * hd] for h in heads], axis=0)
        kg = kv[:, g * hd:(g + 1) * hd]
        vg = kv[:, KV_WIDTH + g * hd:KV_WIDTH + (g + 1) * hd]
        st = lax.dot_general(kg, qg, (((1,), (1,)), ((), ())), preferred_element_type=F32) + bias_ref[0, g]
        sink = jnp.concatenate([jnp.full((1, blk), sink_ref[h], F32) for h in heads], axis=1)
        m = jnp.maximum(jnp.max(st, axis=0, keepdims=True), sink)
        e = jnp.exp(st - m).astype(BF16)
        pv = lax.dot_general(vg, e, (((0,), (0,)), ((), ())), preferred_element_type=F32)
        den = jnp.dot(ones, e, preferred_element_type=F32)[0:1] + jnp.exp(sink - m)
        og = pv / den
        outs.extend(og[:, r * blk:(r + 1) * blk].T for r in range(rep))
    return jnp.concatenate(outs, axis=1)


def _attn_kernel(sink_ref, bias_a_ref, bias_b_ref, q_ref, k0_ref, k1_ref, k2_ref, k3_ref, o_ref):
    blk = ATTN_BLOCK
    k0, k1, k2, k3 = k0_ref[0], k1_ref[0], k2_ref[0], k3_ref[0]
    o_a = _attn_block(q_ref[0, :blk, :], jnp.concatenate([k0, k1, k2], axis=0), bias_a_ref, sink_ref)
    o_b = _attn_block(q_ref[0, blk:, :], jnp.concatenate([k1, k2, k3], axis=0), bias_b_ref, sink_ref)
    o_ref[0, :blk, :] = o_a.astype(o_ref.dtype)
    o_ref[0, blk:, :] = o_b.astype(o_ref.dtype)


def _alibi_window_bias():
    blk = ATTN_BLOCK
    rep = N_Q_HEADS // N_KV_HEADS
    krel = jnp.arange(3 * blk)[:, None] - blk
    qpos = jnp.arange(blk)[None, :]
    dist = jnp.abs(krel - qpos)
    slopes = jnp.asarray([2.0 ** (-8.0 * (h + 1) / N_Q_HEADS) for h in range(N_Q_HEADS)], F32)
    bias = jnp.where(dist <= WINDOW, -slopes[:, None, None] * dist.astype(F32), -jnp.inf)
    bias = bias.reshape(N_KV_HEADS, rep, 3 * blk, blk).transpose(0, 2, 1, 3).reshape(N_KV_HEADS, 3 * blk, rep * blk)
    no_prev = jnp.where(krel >= 0, 0.0, -jnp.inf)
    no_next = jnp.where(krel < blk, 0.0, -jnp.inf)
    return jnp.stack([bias, bias + no_prev, bias + no_next])


def _attention(proj3, sink):
    b, s, _ = proj3.shape
    nb = s // ATTN_BLOCK
    nb2 = nb // 2
    kv_col = ATTN_WIDTH // (2 * KV_WIDTH)
    kv_spec = lambda off: pl.BlockSpec(
        (1, ATTN_BLOCK, 2 * KV_WIDTH), lambda i, j: (i, jnp.clip(2 * j + off, 0, nb - 1), kv_col))
    bias = _alibi_window_bias()
    bias_spec = lambda f: pl.BlockSpec((1,) + bias.shape[1:], f)
    return pl.pallas_call(
        _attn_kernel,
        grid=(b, nb2),
        in_specs=[
            pl.BlockSpec(memory_space=pltpu.SMEM),
            bias_spec(lambda i, j: (jnp.where(j == 0, 1, 0), 0, 0, 0)),
            bias_spec(lambda i, j: (jnp.where(j == nb2 - 1, 2, 0), 0, 0, 0)),
            pl.BlockSpec((1, 2 * ATTN_BLOCK, ATTN_WIDTH), lambda i, j: (i, j, 0)),
            kv_spec(-1), kv_spec(0), kv_spec(1), kv_spec(2),
        ],
        out_specs=pl.BlockSpec((1, 2 * ATTN_BLOCK, ATTN_WIDTH), lambda i, j: (i, j, 0)),
        out_shape=jax.ShapeDtypeStruct((b, s, ATTN_WIDTH), BF16),
        compiler_params=_params("parallel", "parallel"),
    )(sink, bias, bias, proj3, proj3, proj3, proj3, proj3)


def _hgrn_group(q, v, z, lb, state_t, tri, forward):
    g, c, dk = q.shape
    f = lb + (1.0 - lb) * jax.nn.sigmoid(z)
    log_f = jnp.log(f)
    k = 1.0 - f
    tri_g = jnp.broadcast_to(tri[None], (g, c, c))
    pieces = jnp.concatenate(_split3(log_f), axis=2)
    cum3 = jnp.einsum("gts,gsd->gtd", tri_g, pieces, preferred_element_type=F32)
    cum = cum3[:, :, :dk] + cum3[:, :, dk:2 * dk] + cum3[:, :, 2 * dk:]
    if forward:
        ref = cum[:, c // 2:c // 2 + 1, :]
        last = cum[:, c - 1:c, :]
    else:
        ref = cum[:, c - 1 - c // 2:c - c // 2, :]
        last = cum[:, 0:1, :]
    q_rel = (q * jnp.exp(cum - ref)).astype(BF16)
    k_rel = (k * jnp.exp(ref - cum)).astype(BF16)
    a = jnp.einsum("gtd,gsd->gts", q_rel, k_rel, preferred_element_type=F32)
    a = jnp.where(tri_g > 0, a, 0.0).astype(BF16)
    o_intra = jnp.einsum("gts,gsv->gtv", a, v, preferred_element_type=F32)
    k_dec = (k * jnp.exp(last - cum)).astype(BF16)
    kv_t = jnp.einsum("gsv,gsd->gvd", v, k_dec, preferred_element_type=F32)
    q_dec = (q * jnp.exp(cum)).astype(BF16)
    decay = jnp.exp(last)
    o_inter = [None] * g
    for j in (range(g) if forward else range(g - 1, -1, -1)):
        o_inter[j] = lax.dot_general(q_dec[j], state_t.astype(BF16), (((1,), (1,)), ((), ())),
                                     preferred_element_type=F32)
        state_t = state_t * decay[j] + kv_t[j]
    return o_intra + jnp.stack(o_inter, axis=0), state_t


def _hgrn_kernel(q_ref, v_ref, zf_ref, zb_ref, zg_ref, lb_ref, ng_ref, o_ref, of_ref, ob_ref):
    c = HGRN_CHUNK
    g = HGRN_GROUP
    seq, dk = q_ref.shape[1], q_ref.shape[2]
    dv = v_ref.shape[2]
    n_groups = seq // (g * c)
    row = lax.broadcasted_iota(jnp.int32, (c, c), 0)
    col = lax.broadcasted_iota(jnp.int32, (c, c), 1)
    lower = jnp.where(row >= col, 1.0, 0.0).astype(BF16)
    upper = jnp.where(row <= col, 1.0, 0.0).astype(BF16)
    lb_f = lb_ref[0:1, :]
    lb_b = lb_ref[1:2, :]

    def group(ref, sl, width):
        return ref[0, sl, :].astype(F32).reshape(g, c, width)

    def body(i, carry):
        st_f, st_b = carry
        sl_f = pl.ds(pl.multiple_of(i * (g * c), g * c), g * c)
        sl_b = pl.ds(pl.multiple_of((n_groups - 1 - i) * (g * c), g * c), g * c)
        o_f, st_f = _hgrn_group(group(q_ref, sl_f, dk), v_ref[0, sl_f, :].reshape(g, c, dv),
                                group(zf_ref, sl_f, dk), lb_f, st_f, lower, True)
        o_b, st_b = _hgrn_group(group(q_ref, sl_b, dk), v_ref[0, sl_b, :].reshape(g, c, dv),
                                group(zb_ref, sl_b, dk), lb_b, st_b, upper, False)
        of_ref[sl_f, :] = o_f.reshape(g * c, dv)
        ob_ref[sl_b, :] = o_b.reshape(g * c, dv)
        return st_f, st_b

    zero = jnp.zeros((dv, dk), F32)
    lax.fori_loop(0, n_groups, body, (zero, zero))

    o = of_ref[...] + ob_ref[...]
    o = o * lax.rsqrt(jnp.mean(o * o, axis=-1, keepdims=True) + RMS_EPS)
    zg = zg_ref[0].astype(F32)
    o_ref[0] = (o * ng_ref[...] * (zg * jax.nn.sigmoid(zg))).astype(o_ref.dtype)


def _hgrn(proj3, lb, norm_g):
    b, s, _ = proj3.shape
    hd = HGRN_HEAD_DIM
    base = (ATTN_WIDTH + 2 * KV_WIDTH) // hd
    col_spec = lambda part: pl.BlockSpec((1, s, hd), lambda i, h: (i, 0, base + part * N_HGRN_HEADS + h))
    return pl.pallas_call(
        _hgrn_kernel,
        grid=(b, N_HGRN_HEADS),
        in_specs=[
            col_spec(0), col_spec(1), col_spec(2), col_spec(3), col_spec(4),
            pl.BlockSpec((2, hd), lambda i, h: (0, h)),
            pl.BlockSpec((1, hd), lambda i, h: (0, h)),
        ],
        out_specs=pl.BlockSpec((1, s, hd), lambda i, h: (i, 0, h)),
        out_shape=jax.ShapeDtypeStruct((b, s, HGRN_WIDTH), BF16),
        scratch_shapes=[pltpu.VMEM((s, hd), F32), pltpu.VMEM((s, hd), F32)],
        compiler_params=_params("parallel", "parallel"),
    )(proj3, proj3, proj3, proj3, proj3, lb, norm_g)


def _outproj_kernel(x_ref, attn_ref, rec_ref, wo_ref, g0_ref, b0_ref, g1_ref, b1_ref,
                    wr_ref, br_ref, h1_ref, pk_ref, ti_ref, gt_ref):
    half = attn_ref.shape[1]
    h0 = _layer_norm(x_ref[...], g0_ref[...], b0_ref[...])
    mix = (jnp.dot(attn_ref[...], wo_ref[:half, :], preferred_element_type=F32)
           + jnp.dot(rec_ref[...], wo_ref[half:, :], preferred_element_type=F32))
    h1 = _layer_norm(DEEPNORM_ALPHA * h0 + mix, g1_ref[...], b1_ref[...])
    h1_ref[...] = h1

    pk_ref[...] = _pack_bf16_pairs(h1)

    h_hi = h1.astype(BF16)
    h_lo = (h1 - h_hi.astype(F32)).astype(BF16)
    w_hi = wr_ref[0]
    w_lo = wr_ref[1]
    logits = (jnp.dot(h_hi, w_hi, preferred_element_type=F32)
              + jnp.dot(h_hi, w_lo, preferred_element_type=F32)
              + jnp.dot(h_lo, w_hi, preferred_element_type=F32)) + br_ref[...]

    n_exp = logits.shape[1]
    lane = lax.broadcasted_iota(jnp.int32, logits.shape, 1)
    vals, idxs = [], []
    cur = logits
    for _ in range(TOP_K):
        m = jnp.max(cur, axis=-1, keepdims=True)
        idx = jnp.min(jnp.where(cur == m, lane, n_exp), axis=-1, keepdims=True)
        vals.append(m)
        idxs.append(idx)
        cur = jnp.where(lane == idx, -jnp.inf, cur)
    top_v = jnp.concatenate(vals, axis=1)
    e = jnp.exp(top_v - vals[0])
    gt_ref[...] = e / jnp.sum(e, axis=-1, keepdims=True)
    ti_ref[...] = jnp.concatenate(idxs, axis=1)


def _outproj(x2, attn2, rec2, wo_bf16, g0, b0, g1, b1, wr2, br):
    n, d = x2.shape
    half = attn2.shape[1]
    n_exp = br.shape[1]
    tile = lambda w: pl.BlockSpec((TOKEN_TILE, w), lambda i: (i, 0))
    vec = pl.BlockSpec((1, d), lambda i: (0, 0))
    return pl.pallas_call(
        _outproj_kernel,
        grid=(n // TOKEN_TILE,),
        in_specs=[
            tile(d), tile(half), tile(half),
            pl.BlockSpec((d, d), lambda i: (0, 0)),
            vec, vec, vec, vec,
            pl.BlockSpec((2, d, n_exp), lambda i: (0, 0, 0)),
            pl.BlockSpec((1, n_exp), lambda i: (0, 0)),
        ],
        out_specs=[tile(d), tile(d // 2), tile(TOP_K), tile(TOP_K)],
        out_shape=[
            jax.ShapeDtypeStruct((n, d), F32),
            jax.ShapeDtypeStruct((n, d // 2), jnp.uint32),
            jax.ShapeDtypeStruct((n, TOP_K), jnp.int32),
            jax.ShapeDtypeStruct((n, TOP_K), F32),
        ],
        compiler_params=_params("parallel"),
    )(x2, attn2, rec2, wo_bf16, g0, b0, g1, b1, wr2, br)


def _rank_kernel(ti_ref, rank_ref, cnt_ref, run_ref, *, n_exp):
    i = pl.program_id(0)

    @pl.when(i == 0)
    def _():
        run_ref[...] = jnp.zeros_like(run_ref)

    ti = ti_ref[...]
    t = ti.shape[0]
    lane = lax.broadcasted_iota(jnp.int32, (t, n_exp), 1)
    hots = [jnp.where(lane == ti[:, k:k + 1], 1.0, 0.0) for k in range(TOP_K)]
    member = hots[0]
    for k in range(1, TOP_K):
        member = member + hots[k]
    row = lax.broadcasted_iota(jnp.int32, (t, t), 0)
    col = lax.broadcasted_iota(jnp.int32, (t, t), 1)
    strict_lower = jnp.where(row > col, 1.0, 0.0).astype(BF16)
    before = jnp.dot(strict_lower, member.astype(BF16), preferred_element_type=F32) + run_ref[...]
    ranks = [jnp.sum(hots[k] * before, axis=-1, keepdims=True) for k in range(TOP_K)]
    cols = jnp.concatenate(ranks + [ti.astype(F32), jnp.zeros((t, 128 - 2 * TOP_K), F32)], axis=1)
    rank_ref[...] = cols.T[:2 * TOP_K, :].astype(jnp.int32)
    run_ref[...] = run_ref[...] + jnp.sum(member, axis=0, keepdims=True)
    cnt_ref[...] = run_ref[...].astype(jnp.int32)


def _ranks(top_i, n_exp):
    n = top_i.shape[0]
    return pl.pallas_call(
        functools.partial(_rank_kernel, n_exp=n_exp),
        grid=(n // TOKEN_TILE,),
        in_specs=[pl.BlockSpec((TOKEN_TILE, TOP_K), lambda i: (i, 0))],
        out_specs=[pl.BlockSpec((2 * TOP_K, TOKEN_TILE), lambda i: (0, i)),
                   pl.BlockSpec((1, n_exp), lambda i: (0, 0))],
        out_shape=[jax.ShapeDtypeStruct((2 * TOP_K, n), jnp.int32),
                   jax.ShapeDtypeStruct((1, n_exp), jnp.int32)],
        scratch_shapes=[pltpu.VMEM((1, n_exp), F32)],
        compiler_params=_params("arbitrary"),
    )(top_i)


def _row_copy(src_ref, src_row, dst_ref, dst_row, sem):
    return pltpu.make_async_copy(src_ref.at[pl.ds(src_row, 1), :], dst_ref.at[pl.ds(dst_row, 1), :], sem)


def _scatter_split_kernel(dest_ref, x_ref, w_ref, init_ref, xs_ref, wg_ref, wl_ref, sem, *, blk):
    del init_ref
    t = x_ref.shape[0]

    def copies(i):
        for u in range(ROW_UNROLL):
            r = i * ROW_UNROLL + u
            for k in range(TOP_K):
                yield _row_copy(x_ref, r, xs_ref, dest_ref[k, r], sem), (u * TOP_K + k) % 2

    def issue(i, carry):
        for cp, prio in copies(i):
            cp.start(priority=prio)
        return carry

    def drain(i, carry):
        for cp, _ in copies(i):
            cp.wait()
        return carry

    if t == ROW_UNROLL:
        issue(0, 0)
    else:
        lax.fori_loop(0, t // ROW_UNROLL, issue, 0)
    _split_up_columns(w_ref, wg_ref, wl_ref, blk)
    if t == ROW_UNROLL:
        drain(0, 0)
    else:
        lax.fori_loop(0, t // ROW_UNROLL, drain, 0)


def _split_up_columns(w_ref, wg_ref, wl_ref, blk):
    r = lax.broadcasted_iota(jnp.int32, (2 * blk, blk), 0)
    c = lax.broadcasted_iota(jnp.int32, (2 * blk, blk), 1)
    pick_even = jnp.where(r == 2 * c, 1.0, 0.0).astype(BF16)
    pick_odd = jnp.where(r == 2 * c + 1, 1.0, 0.0).astype(BF16)
    for j in range(wg_ref.shape[2] // blk):
        w = w_ref[0, :, 2 * blk * j:2 * blk * (j + 1)].astype(BF16)
        wg_ref[0, :, blk * j:blk * (j + 1)] = jnp.dot(w, pick_even, preferred_element_type=F32).astype(BF16)
        wl_ref[0, :, blk * j:blk * (j + 1)] = jnp.dot(w, pick_odd, preferred_element_type=F32).astype(BF16)


def _scatter_rows_split_weights(dest, packed, n_rows, w_up):
    n, w = packed.shape
    n_exp, d, de2 = w_up.shape
    de = de2 // 2
    steps = n // ROUTE_TILE
    slabs = steps // n_exp
    assert steps == slabs * n_exp and de % (slabs * SPLIT_BLOCK) == 0
    init = jnp.zeros((n_rows, w), packed.dtype)
    w_out = jax.ShapeDtypeStruct((n_exp, d, de), BF16)
    slab = lambda width: pl.BlockSpec((1, d, width), lambda i: (i // slabs, 0, i % slabs))
    return pl.pallas_call(
        functools.partial(_scatter_split_kernel, blk=SPLIT_BLOCK),
        grid=(steps,),
        in_specs=[
            pl.BlockSpec((TOP_K, ROUTE_TILE), lambda i: (0, i), memory_space=pltpu.SMEM),
            pl.BlockSpec((ROUTE_TILE, w), lambda i: (i, 0)),
            slab(de2 // slabs),
            pl.BlockSpec(memory_space=pl.ANY),
        ],
        out_specs=[pl.BlockSpec(memory_space=pl.ANY), slab(de // slabs), slab(de // slabs)],
        out_shape=[jax.ShapeDtypeStruct((n_rows, w), packed.dtype), w_out, w_out],
        scratch_shapes=[pltpu.SemaphoreType.DMA(())],
        input_output_aliases={3: 0},
        compiler_params=_params("arbitrary"),
    )(dest, packed, w_up, init)


def _ffn_kernel(te_ref, nv_ref, xs_ref, wg_ref, wl_ref, bg_ref, bl_ref, wd_ref, bd_ref, ys_ref):
    i = pl.program_id(0)

    @pl.when(i < nv_ref[0])
    def _():
        lo, hi = _unpack_bf16_pairs(xs_ref[...])
        x = jnp.concatenate([lo.astype(BF16), hi.astype(BF16)], axis=1)
        hg = jnp.dot(x, wg_ref[0], preferred_element_type=F32) + bg_ref[0]
        hl = jnp.dot(x, wl_ref[0], preferred_element_type=F32) + bl_ref[0]
        xg = jnp.minimum(hg, SWIGLU_LIMIT)
        xl = jnp.clip(hl, -SWIGLU_LIMIT, SWIGLU_LIMIT)
        act = xg * jax.nn.sigmoid(SWIGLU_ALPHA * xg) * (xl + 1.0)
        ys_ref[...] = jnp.dot(act.astype(BF16), wd_ref[0].astype(BF16), preferred_element_type=F32) + bd_ref[0]

    @pl.when(i >= nv_ref[0])
    def _():
        ys_ref[...] = jnp.zeros_like(ys_ref)


def _grouped_ffn(tile_expert, n_valid, xs, wg, wl, bg, bl, wd, bd):
    n_rows, w = xs.shape
    n_exp, d, de = wg.shape
    n_tiles = n_rows // FFN_TILE
    row = lambda i, te, nv: (jnp.minimum(i, nv[0] - 1), 0)
    exp3 = lambda i, te, nv: (te[i], 0, 0)
    grid_spec = pltpu.PrefetchScalarGridSpec(
        num_scalar_prefetch=2,
        grid=(n_tiles,),
        in_specs=[
            pl.BlockSpec((FFN_TILE, w), row),
            pl.BlockSpec((1, d, de), exp3),
            pl.BlockSpec((1, d, de), exp3),
            pl.BlockSpec((1, 1, de), exp3),
            pl.BlockSpec((1, 1, de), exp3),
            pl.BlockSpec((1, de, d), exp3),
            pl.BlockSpec((1, 1, d), exp3),
        ],
        out_specs=pl.BlockSpec((FFN_TILE, d), lambda i, te, nv: (i, 0)),
    )
    return pl.pallas_call(
        _ffn_kernel,
        grid_spec=grid_spec,
        out_shape=jax.ShapeDtypeStruct((n_rows, d), F32),
        compiler_params=_params("arbitrary"),
    )(tile_expert, n_valid, xs, wg, wl, bg, bl, wd, bd)


def _combine_kernel(dest_ref, gt_ref, h1_ref, g_ref, b_ref, ys_ref, o_ref, buf_ref, sem):
    t = h1_ref.shape[0]

    def copies(i):
        for u in range(ROW_UNROLL):
            r = i * ROW_UNROLL + u
            for k in range(TOP_K):
                yield _row_copy(ys_ref, dest_ref[k, r], buf_ref.at[k], r, sem), (u * TOP_K + k) % 2

    def issue(i, carry):
        for cp, prio in copies(i):
            cp.start(priority=prio)
        return carry

    def drain(i, carry):
        for cp, _ in copies(i):
            cp.wait()
        return carry

    if t == ROW_UNROLL:
        issue(0, 0)
        drain(0, 0)
    else:
        lax.fori_loop(0, t // ROW_UNROLL, issue, 0)
        lax.fori_loop(0, t // ROW_UNROLL, drain, 0)

    gates = gt_ref[...]
    ffn = gates[:, 0:1] * buf_ref[0]
    for k in range(1, TOP_K):
        ffn = ffn + gates[:, k:k + 1] * buf_ref[k]
    o_ref[...] = _layer_norm(DEEPNORM_ALPHA * h1_ref[...] + ffn, g_ref[...], b_ref[...])


def _combine(dest, gates, h1, g, b, ys):
    n, d = h1.shape
    return pl.pallas_call(
        _combine_kernel,
        grid=(n // ROUTE_TILE,),
        in_specs=[
            pl.BlockSpec((TOP_K, ROUTE_TILE), lambda i: (0, i), memory_space=pltpu.SMEM),
            pl.BlockSpec((ROUTE_TILE, TOP_K), lambda i: (i, 0)),
            pl.BlockSpec((ROUTE_TILE, d), lambda i: (i, 0)),
            pl.BlockSpec((1, d), lambda i: (0, 0)),
            pl.BlockSpec((1, d), lambda i: (0, 0)),
            pl.BlockSpec(memory_space=pl.ANY),
        ],
        out_specs=pl.BlockSpec((ROUTE_TILE, d), lambda i: (i, 0)),
        out_shape=jax.ShapeDtypeStruct((n, d), F32),
        scratch_shapes=[pltpu.VMEM((TOP_K, ROUTE_TILE, d), F32), pltpu.SemaphoreType.DMA(())],
        compiler_params=_params("arbitrary"),
    )(dest, gates, h1, g, b, ys)


def kernel(x, emb_ln_g, emb_ln_b, w_in, attn_sink, hgrn_lb_logits, hgrn_norm_g, w_out, ln1_g, ln1_b,
           w_router, b_router, w_up, b_up, w_down, b_down, ln2_g, ln2_b):
    bsz, seq, d = x.shape
    n = bsz * seq
    n_exp = w_router.shape[-1]
    assert w_in.shape[0] == DEPTH == 1
    assert seq % (2 * ATTN_BLOCK) == 0 and seq % (HGRN_CHUNK * HGRN_GROUP) == 0
    assert n % TOKEN_TILE == 0 and n % ROUTE_TILE == 0
    row = lambda v: v.reshape(1, -1).astype(F32)

    x2 = x.reshape(n, d)
    g0, b0 = row(emb_ln_g), row(emb_ln_b)

    proj = _inproj(x2, g0, b0, w_in[0].astype(BF16))
    proj3 = proj.reshape(bsz, seq, -1)
    attn = _attention(proj3, attn_sink[0].astype(F32))

    lb = jnp.cumsum(jax.nn.softmax(hgrn_lb_logits.astype(F32), axis=1), axis=1)[:, 0]
    rec = _hgrn(proj3, lb, row(hgrn_norm_g[0]))

    wr = w_router[0].astype(F32)
    wr_hi = wr.astype(BF16)
    wr_lo = (wr - wr_hi.astype(F32)).astype(BF16)
    h1, packed, top_i, gates = _outproj(
        x2, attn.reshape(n, -1), rec.reshape(n, -1), w_out[0].astype(BF16),
        g0, b0, row(ln1_g[0]), row(ln1_b[0]), jnp.stack([wr_hi, wr_lo]), row(b_router[0]))

    rank_ti, counts = _ranks(top_i, n_exp)
    counts = counts[0]
    padded = ((counts + FFN_TILE - 1) // FFN_TILE) * FFN_TILE
    ends = jnp.cumsum(padded)
    starts = ends - padded
    n_tiles = (n * TOP_K) // FFN_TILE + n_exp
    n_rows = n_tiles * FFN_TILE
    dest = starts[rank_ti[TOP_K:]] + rank_ti[:TOP_K]
    tile_start = jnp.arange(n_tiles, dtype=jnp.int32) * FFN_TILE
    tile_expert = jnp.minimum(
        jnp.sum(tile_start[:, None] >= ends[None, :], axis=1), n_exp - 1).astype(jnp.int32)
    n_valid = (ends[-1:] // FFN_TILE).astype(jnp.int32)

    xs, wg, wl = _scatter_rows_split_weights(dest, packed, n_rows, w_up[0].astype(F32))
    bu = b_up[0].astype(F32)
    ys = _grouped_ffn(
        tile_expert, n_valid, xs, wg, wl,
        bu[:, None, 0::2], bu[:, None, 1::2],
        w_down[0].astype(F32), b_down[0].astype(F32)[:, None, :])

    out = _combine(dest, gates, h1, row(ln2_g[0]), row(ln2_b[0]), ys)
    return out.reshape(bsz, seq, d)
```

```python
import functools
import math

import jax
import jax.numpy as jnp
from jax import lax
from jax.experimental import pallas as pl
from jax.experimental.pallas import tpu as pltpu

F32 = jnp.float32
BF16 = jnp.bfloat16

ATTN_HEAD_DIM = 64
N_Q_HEADS = 8
N_KV_HEADS = 2
ATTN_WIDTH = N_Q_HEADS * ATTN_HEAD_DIM
KV_WIDTH = N_KV_HEADS * ATTN_HEAD_DIM
WINDOW = 128
ATTN_BLOCK = 128
HGRN_HEAD_DIM = 128
N_HGRN_HEADS = 4
HGRN_WIDTH = N_HGRN_HEADS * HGRN_HEAD_DIM
HGRN_CHUNK = 64
TOP_K = 4
SWIGLU_LIMIT = 7.0
SWIGLU_ALPHA = 1.702
LN_EPS = 1e-5
RMS_EPS = 1e-6
DEPTH = 1
DEEPNORM_ALPHA = (2.0 * DEPTH) ** 0.25

VMEM_LIMIT_BYTES = 48 * 1024 * 1024
TOKEN_TILE = 512
ROW_PARTS = 4
ROUTE_TILE = 256
FFN_TILE = 512
SPLIT_BLOCK = 256
ROW_UNROLL = 256
HGRN_GROUP = 32


def _params(*sem):
    return pltpu.CompilerParams(dimension_semantics=sem, vmem_limit_bytes=VMEM_LIMIT_BYTES)


def _layer_norm(x, g, b):
    mu = jnp.mean(x, axis=-1, keepdims=True)
    xc = x - mu
    var = jnp.mean(xc * xc, axis=-1, keepdims=True)
    return xc * lax.rsqrt(var + LN_EPS) * g + b


def _split3(x):
    hi = x.astype(BF16)
    r = x - hi.astype(F32)
    mid = r.astype(BF16)
    lo = (r - mid.astype(F32)).astype(BF16)
    return hi, mid, lo


def _pack_bf16_pairs(x):
    d2 = x.shape[1] // 2
    lo_bits = lax.bitcast_convert_type(x[:, :d2].astype(BF16).astype(F32), jnp.uint32)
    hi_bits = lax.bitcast_convert_type(x[:, d2:].astype(BF16).astype(F32), jnp.uint32)
    return hi_bits | (lo_bits >> 16)


def _unpack_bf16_pairs(u):
    lo = lax.bitcast_convert_type(u << 16, F32)
    hi = lax.bitcast_convert_type(u & jnp.uint32(0xFFFF0000), F32)
    return lo, hi


def _inproj_kernel(x_ref, g_ref, b_ref, w_ref, o_ref, *, col_chunk):
    n_rows, n_cols = o_ref.shape
    part = n_rows // ROW_PARTS
    for r in range(0, n_rows, part):
        h = _layer_norm(x_ref[r:r + part, :], g_ref[...], b_ref[...]).astype(BF16)
        for c in range(0, n_cols, col_chunk):
            o_ref[r:r + part, c:c + col_chunk] = jnp.dot(
                h, w_ref[:, c:c + col_chunk], preferred_element_type=F32).astype(o_ref.dtype)


def _inproj(x2, g, b, w_bf16):
    n, d = x2.shape
    cols = w_bf16.shape[1]
    return pl.pallas_call(
        functools.partial(_inproj_kernel, col_chunk=256),
        grid=(n // TOKEN_TILE,),
        in_specs=[
            pl.BlockSpec((TOKEN_TILE, d), lambda i: (i, 0)),
            pl.BlockSpec((1, d), lambda i: (0, 0)),
            pl.BlockSpec((1, d), lambda i: (0, 0)),
            pl.BlockSpec((d, cols), lambda i: (0, 0)),
        ],
        out_specs=pl.BlockSpec((TOKEN_TILE, cols), lambda i: (i, 0)),
        out_shape=jax.ShapeDtypeStruct((n, cols), BF16),
        compiler_params=_params("parallel"),
    )(x2, g, b, w_bf16)


def _attn_block(q, kv, bias_ref, sink_ref):
    blk, hd = ATTN_BLOCK, ATTN_HEAD_DIM
    rep = N_Q_HEADS // N_KV_HEADS
    q = q * (1.0 / math.sqrt(hd))
    ones = jnp.ones((8, 3 * blk), BF16)
    outs = []
    for g in range(N_KV_HEADS):
        heads = range(g * rep, (g + 1) * rep)
        qg = jnp.concatenate([q[:, h * hd:(h + 1) * hd] for h in heads], axis=0)
        kg = kv[:, g * hd:(g + 1) * hd]
        vg = kv[:, KV_WIDTH + g * hd:KV_WIDTH + (g + 1) * hd]
        st = lax.dot_general(kg, qg, (((1,), (1,)), ((), ())), preferred_element_type=F32) + bias_ref[0, g]
        sink = jnp.concatenate([jnp.full((1, blk), sink_ref[h], F32) for h in heads], axis=1)
        m = jnp.maximum(jnp.max(st, axis=0, keepdims=True), sink)
        e = jnp.exp(st - m).astype(BF16)
        pv = lax.dot_general(vg, e, (((0,), (0,)), ((), ())), preferred_element_type=F32)
        den = jnp.dot(ones, e, preferred_element_type=F32)[0:1] + jnp.exp(sink - m)
        og = pv / den
        outs.extend(og[:, r * blk:(r + 1) * blk].T for r in range(rep))
    return jnp.concatenate(outs, axis=1)


def _attn_kernel(sink_ref, bias_a_ref, bias_b_ref, q_ref, k0_ref, k1_ref, k2_ref, k3_ref, o_ref):
    blk = ATTN_BLOCK
    k0, k1, k2, k3 = k0_ref[0], k1_ref[0], k2_ref[0], k3_ref[0]
    o_a = _attn_block(q_ref[0, :blk, :], jnp.concatenate([k0, k1, k2], axis=0), bias_a_ref, sink_ref)
    o_b = _attn_block(q_ref[0, blk:, :], jnp.concatenate([k1, k2, k3], axis=0), bias_b_ref, sink_ref)
    o_ref[0, :blk, :] = o_a.astype(o_ref.dtype)
    o_ref[0, blk:, :] = o_b.astype(o_ref.dtype)


def _alibi_window_bias():
    blk = ATTN_BLOCK
    rep = N_Q_HEADS // N_KV_HEADS
    krel = jnp.arange(3 * blk)[:, None] - blk
    qpos = jnp.arange(blk)[None, :]
    dist = jnp.abs(krel - qpos)
    slopes = jnp.asarray([2.0 ** (-8.0 * (h + 1) / N_Q_HEADS) for h in range(N_Q_HEADS)], F32)
    bias = jnp.where(dist <= WINDOW, -slopes[:, None, None] * dist.astype(F32), -jnp.inf)
    bias = bias.reshape(N_KV_HEADS, rep, 3 * blk, blk).transpose(0, 2, 1, 3).reshape(N_KV_HEADS, 3 * blk, rep * blk)
    no_prev = jnp.where(krel >= 0, 0.0, -jnp.inf)
    no_next = jnp.where(krel < blk, 0.0, -jnp.inf)
    return jnp.stack([bias, bias + no_prev, bias + no_next])


def _attention(proj3, sink):
    b, s, _ = proj3.shape
    nb = s // ATTN_BLOCK
    nb2 = nb // 2
    kv_col = ATTN_WIDTH // (2 * KV_WIDTH)
    kv_spec = lambda off: pl.BlockSpec(
        (1, ATTN_BLOCK, 2 * KV_WIDTH), lambda i, j: (i, jnp.clip(2 * j + off, 0, nb - 1), kv_col))
    bias = _alibi_window_bias()
    bias_spec = lambda f: pl.BlockSpec((1,) + bias.shape[1:], f)
    return pl.pallas_call(
        _attn_kernel,
        grid=(b, nb2),
        in_specs=[
            pl.BlockSpec(memory_space=pltpu.SMEM),
            bias_spec(lambda i, j: (jnp.where(j == 0, 1, 0), 0, 0, 0)),
            bias_spec(lambda i, j: (jnp.where(j == nb2 - 1, 2, 0), 0, 0, 0)),
            pl.BlockSpec((1, 2 * ATTN_BLOCK, ATTN_WIDTH), lambda i, j: (i, j, 0)),
            kv_spec(-1), kv_spec(0), kv_spec(1), kv_spec(2),
        ],
        out_specs=pl.BlockSpec((1, 2 * ATTN_BLOCK, ATTN_WIDTH), lambda i, j: (i, j, 0)),
        out_shape=jax.ShapeDtypeStruct((b, s, ATTN_WIDTH), BF16),
        compiler_params=_params("parallel", "parallel"),
    )(sink, bias, bias, proj3, proj3, proj3, proj3, proj3)


def _hgrn_group(q, v, z, lb, state_t, tri, forward):
    g, c, dk = q.shape
    f = lb + (1.0 - lb) * jax.nn.sigmoid(z)
    log_f = jnp.log(f)
    k = 1.0 - f
    tri_g = jnp.broadcast_to(tri[None], (g, c, c))
    pieces = jnp.concatenate(_split3(log_f), axis=2)
    cum3 = jnp.einsum("gts,gsd->gtd", tri_g, pieces, preferred_element_type=F32)
    cum = cum3[:, :, :dk] + cum3[:, :, dk:2 * dk] + cum3[:, :, 2 * dk:]
    if forward:
        ref = cum[:, c // 2:c // 2 + 1, :]
        last = cum[:, c - 1:c, :]
    else:
        ref = cum[:, c - 1 - c // 2:c - c // 2, :]
        last = cum[:, 0:1, :]
    q_rel = (q * jnp.exp(cum - ref)).astype(BF16)
    k_rel = (k * jnp.exp(ref - cum)).astype(BF16)
    a = jnp.einsum("gtd,gsd->gts", q_rel, k_rel, preferred_element_type=F32)
    a = jnp.where(tri_g > 0, a, 0.0).astype(BF16)
    o_intra = jnp.einsum("gts,gsv->gtv", a, v, preferred_element_type=F32)
    k_dec = (k * jnp.exp(last - cum)).astype(BF16)
    kv_t = jnp.einsum("gsv,gsd->gvd", v, k_dec, preferred_element_type=F32)
    q_dec = (q * jnp.exp(cum)).astype(BF16)
    decay = jnp.exp(last)
    o_inter = [None] * g
    for j in (range(g) if forward else range(g - 1, -1, -1)):
        o_inter[j] = lax.dot_general(q_dec[j], state_t.astype(BF16), (((1,), (1,)), ((), ())),
                                     preferred_element_type=F32)
        state_t = state_t * decay[j] + kv_t[j]
    return o_intra + jnp.stack(o_inter, axis=0), state_t


def _hgrn_kernel(q_ref, v_ref, zf_ref, zb_ref, zg_ref, lb_ref, ng_ref, o_ref, of_ref, ob_ref):
    c = HGRN_CHUNK
    g = HGRN_GROUP
    seq, dk = q_ref.shape[1], q_ref.shape[2]
    dv = v_ref.shape[2]
    n_groups = seq // (g * c)
    row = lax.broadcasted_iota(jnp.int32, (c, c), 0)
    col = lax.broadcasted_iota(jnp.int32, (c, c), 1)
    lower = jnp.where(row >= col, 1.0, 0.0).astype(BF16)
    upper = jnp.where(row <= col, 1.0, 0.0).astype(BF16)
    lb_f = lb_ref[0:1, :]
    lb_b = lb_ref[1:2, :]

    def group(ref, sl, width):
        return ref[0, sl, :].astype(F32).reshape(g, c, width)

    def body(i, carry):
        st_f, st_b = carry
        sl_f = pl.ds(pl.multiple_of(i * (g * c), g * c), g * c)
        sl_b = pl.ds(pl.multiple_of((n_groups - 1 - i) * (g * c), g * c), g * c)
        o_f, st_f = _hgrn_group(group(q_ref, sl_f, dk), v_ref[0, sl_f, :].reshape(g, c, dv),
                                group(zf_ref, sl_f, dk), lb_f, st_f, lower, True)
        o_b, st_b = _hgrn_group(group(q_ref, sl_b, dk), v_ref[0, sl_b, :].reshape(g, c, dv),
                                group(zb_ref, sl_b, dk), lb_b, st_b, upper, False)
        of_ref[sl_f, :] = o_f.reshape(g * c, dv)
        ob_ref[sl_b, :] = o_b.reshape(g * c, dv)
        return st_f, st_b

    zero = jnp.zeros((dv, dk), F32)
    lax.fori_loop(0, n_groups, body, (zero, zero))

    o = of_ref[...] + ob_ref[...]
    o = o * lax.rsqrt(jnp.mean(o * o, axis=-1, keepdims=True) + RMS_EPS)
    zg = zg_ref[0].astype(F32)
    o_ref[0] = (o * ng_ref[...] * (zg * jax.nn.sigmoid(zg))).astype(o_ref.dtype)


def _hgrn(proj3, lb, norm_g):
    b, s, _ = proj3.shape
    hd = HGRN_HEAD_DIM
    base = (ATTN_WIDTH + 2 * KV_WIDTH) // hd
    col_spec = lambda part: pl.BlockSpec((1, s, hd), lambda i, h: (i, 0, base + part * N_HGRN_HEADS + h))
    return pl.pallas_call(
        _hgrn_kernel,
        grid=(b, N_HGRN_HEADS),
        in_specs=[
            col_spec(0), col_spec(1), col_spec(2), col_spec(3), col_spec(4),
            pl.BlockSpec((2, hd), lambda i, h: (0, h)),
            pl.BlockSpec((1, hd), lambda i, h: (0, h)),
        ],
        out_specs=pl.BlockSpec((1, s, hd), lambda i, h: (i, 0, h)),
        out_shape=jax.ShapeDtypeStruct((b, s, HGRN_WIDTH), BF16),
        scratch_shapes=[pltpu.VMEM((s, hd), F32), pltpu.VMEM((s, hd), F32)],
        compiler_params=_params("parallel", "parallel"),
    )(proj3, proj3, proj3, proj3, proj3, lb, norm_g)


def _outproj_kernel(x_ref, attn_ref, rec_ref, wo_ref, g0_ref, b0_ref, g1_ref, b1_ref,
                    wr_ref, br_ref, h1_ref, pk_ref, ti_ref, gt_ref):
    half = attn_ref.shape[1]
    h0 = _layer_norm(x_ref[...], g0_ref[...], b0_ref[...])
    mix = (jnp.dot(attn_ref[...], wo_ref[:half, :], preferred_element_type=F32)
           + jnp.dot(rec_ref[...], wo_ref[half:, :], preferred_element_type=F32))
    h1 = _layer_norm(DEEPNORM_ALPHA * h0 + mix, g1_ref[...], b1_ref[...])
    h1_ref[...] = h1

    pk_ref[...] = _pack_bf16_pairs(h1)

    h_hi = h1.astype(BF16)
    h_lo = (h1 - h_hi.astype(F32)).astype(BF16)
    w_hi = wr_ref[0]
    w_lo = wr_ref[1]
    logits = (jnp.dot(h_hi, w_hi, preferred_element_type=F32)
              + jnp.dot(h_hi, w_lo, preferred_element_type=F32)
              + jnp.dot(h_lo, w_hi, preferred_element_type=F32)) + br_ref[...]

    n_exp = logits.shape[1]
    lane = lax.broadcasted_iota(jnp.int32, logits.shape, 1)
    vals, idxs = [], []
    cur = logits
    for _ in range(TOP_K):
        m = jnp.max(cur, axis=-1, keepdims=True)
        idx = jnp.min(jnp.where(cur == m, lane, n_exp), axis=-1, keepdims=True)
        vals.append(m)
        idxs.append(idx)
        cur = jnp.where(lane == idx, -jnp.inf, cur)
    top_v = jnp.concatenate(vals, axis=1)
    e = jnp.exp(top_v - vals[0])
    gt_ref[...] = e / jnp.sum(e, axis=-1, keepdims=True)
    ti_ref[...] = jnp.concatenate(idxs, axis=1)


def _outproj(x2, attn2, rec2, wo_bf16, g0, b0, g1, b1, wr2, br):
    n, d = x2.shape
    half = attn2.shape[1]
    n_exp = br.shape[1]
    tile = lambda w: pl.BlockSpec((TOKEN_TILE, w), lambda i: (i, 0))
    vec = pl.BlockSpec((1, d), lambda i: (0, 0))
    return pl.pallas_call(
        _outproj_kernel,
        grid=(n // TOKEN_TILE,),
        in_specs=[
            tile(d), tile(half), tile(half),
            pl.BlockSpec((d, d), lambda i: (0, 0)),
            vec, vec, vec, vec,
            pl.BlockSpec((2, d, n_exp), lambda i: (0, 0, 0)),
            pl.BlockSpec((1, n_exp), lambda i: (0, 0)),
        ],
        out_specs=[tile(d), tile(d // 2), tile(TOP_K), tile(TOP_K)],
        out_shape=[
            jax.ShapeDtypeStruct((n, d), F32),
            jax.ShapeDtypeStruct((n, d // 2), jnp.uint32),
            jax.ShapeDtypeStruct((n, TOP_K), jnp.int32),
            jax.ShapeDtypeStruct((n, TOP_K), F32),
        ],
        compiler_params=_params("parallel"),
    )(x2, attn2, rec2, wo_bf16, g0, b0, g1, b1, wr2, br)


def _rank_kernel(ti_ref, rank_ref, cnt_ref, run_ref, *, n_exp):
    i = pl.program_id(0)

    @pl.when(i == 0)
    def _():
        run_ref[...] = jnp.zeros_like(run_ref)

    ti = ti_ref[...]
    t = ti.shape[0]
    lane = lax.broadcasted_iota(jnp.int32, (t, n_exp), 1)
    hots = [jnp.where(lane == ti[:, k:k + 1], 1.0, 0.0) for k in range(TOP_K)]
    member = hots[0]
    for k in range(1, TOP_K):
        member = member + hots[k]
    row = lax.broadcasted_iota(jnp.int32, (t, t), 0)
    col = lax.broadcasted_iota(jnp.int32, (t, t), 1)
    strict_lower = jnp.where(row > col, 1.0, 0.0).astype(BF16)
    before = jnp.dot(strict_lower, member.astype(BF16), preferred_element_type=F32) + run_ref[...]
    ranks = [jnp.sum(hots[k] * before, axis=-1, keepdims=True) for k in range(TOP_K)]
    cols = jnp.concatenate(ranks + [ti.astype(F32), jnp.zeros((t, 128 - 2 * TOP_K), F32)], axis=1)
    rank_ref[...] = cols.T[:2 * TOP_K, :].astype(jnp.int32)
    run_ref[...] = run_ref[...] + jnp.sum(member, axis=0, keepdims=True)
    cnt_ref[...] = run_ref[...].astype(jnp.int32)


def _ranks(top_i, n_exp):
    n = top_i.shape[0]
    return pl.pallas_call(
        functools.partial(_rank_kernel, n_exp=n_exp),
        grid=(n // TOKEN_TILE,),
        in_specs=[pl.BlockSpec((TOKEN_TILE, TOP_K), lambda i: (i, 0))],
        out_specs=[pl.BlockSpec((2 * TOP_K, TOKEN_TILE), lambda i: (0, i)),
                   pl.BlockSpec((1, n_exp), lambda i: (0, 0))],
        out_shape=[jax.ShapeDtypeStruct((2 * TOP_K, n), jnp.int32),
                   jax.ShapeDtypeStruct((1, n_exp), jnp.int32)],
        scratch_shapes=[pltpu.VMEM((1, n_exp), F32)],
        compiler_params=_params("arbitrary"),
    )(top_i)


def _row_copy(src_ref, src_row, dst_ref, dst_row, sem):
    return pltpu.make_async_copy(src_ref.at[pl.ds(src_row, 1), :], dst_ref.at[pl.ds(dst_row, 1), :], sem)


def _scatter_split_kernel(dest_ref, x_ref, w_ref, init_ref, xs_ref, wg_ref, wl_ref, sem, *, blk):
    del init_ref
    t = x_ref.shape[0]

    def copies(i):
        for u in range(ROW_UNROLL):
            r = i * ROW_UNROLL + u
            for k in range(TOP_K):
                yield _row_copy(x_ref, r, xs_ref, dest_ref[k, r], sem), (u * TOP_K + k) % 2

    def issue(i, carry):
        for cp, prio in copies(i):
            cp.start(priority=prio)
        return carry

    def drain(i, carry):
        for cp, _ in copies(i):
            cp.wait()
        return carry

    if t == ROW_UNROLL:
        issue(0, 0)
    else:
        lax.fori_loop(0, t // ROW_UNROLL, issue, 0)
    _split_up_columns(w_ref, wg_ref, wl_ref, blk)
    if t == ROW_UNROLL:
        drain(0, 0)
    else:
        lax.fori_loop(0, t // ROW_UNROLL, drain, 0)


def _split_up_columns(w_ref, wg_ref, wl_ref, blk):
    r = lax.broadcasted_iota(jnp.int32, (2 * blk, blk), 0)
    c = lax.broadcasted_iota(jnp.int32, (2 * blk, blk), 1)
    pick_even = jnp.where(r == 2 * c, 1.0, 0.0).astype(BF16)
    pick_odd = jnp.where(r == 2 * c + 1, 1.0, 0.0).astype(BF16)
    for j in range(wg_ref.shape[2] // blk):
        w = w_ref[0, :, 2 * blk * j:2 * blk * (j + 1)].astype(BF16)
        wg_ref[0, :, blk * j:blk * (j + 1)] = jnp.dot(w, pick_even, preferred_element_type=F32).astype(BF16)
        wl_ref[0, :, blk * j:blk * (j + 1)] = jnp.dot(w, pick_odd, preferred_element_type=F32).astype(BF16)


def _scatter_rows_split_weights(dest, packed, n_rows, w_up):
    n, w = packed.shape
    n_exp, d, de2 = w_up.shape
    de = de2 // 2
    steps = n // ROUTE_TILE
    slabs = steps // n_exp
    assert steps == slabs * n_exp and de % (slabs * SPLIT_BLOCK) == 0
    init = jnp.zeros((n_rows, w), packed.dtype)
    w_out = jax.ShapeDtypeStruct((n_exp, d, de), BF16)
    slab = lambda width: pl.BlockSpec((1, d, width), lambda i: (i // slabs, 0, i % slabs))
    return pl.pallas_call(
        functools.partial(_scatter_split_kernel, blk=SPLIT_BLOCK),
        grid=(steps,),
        in_specs=[
            pl.BlockSpec((TOP_K, ROUTE_TILE), lambda i: (0, i), memory_space=pltpu.SMEM),
            pl.BlockSpec((ROUTE_TILE, w), lambda i: (i, 0)),
            slab(de2 // slabs),
            pl.BlockSpec(memory_space=pl.ANY),
        ],
        out_specs=[pl.BlockSpec(memory_space=pl.ANY), slab(de // slabs), slab(de // slabs)],
        out_shape=[jax.ShapeDtypeStruct((n_rows, w), packed.dtype), w_out, w_out],
        scratch_shapes=[pltpu.SemaphoreType.DMA(())],
        input_output_aliases={3: 0},
        compiler_params=_params("arbitrary"),
    )(dest, packed, w_up, init)


def _ffn_kernel(te_ref, nv_ref, xs_ref, wg_ref, wl_ref, bg_ref, bl_ref, wd_ref, bd_ref, ys_ref):
    i = pl.program_id(0)

    @pl.when(i < nv_ref[0])
    def _():
        lo, hi = _unpack_bf16_pairs(xs_ref[...])
        x = jnp.concatenate([lo.astype(BF16), hi.astype(BF16)], axis=1)
        hg = jnp.dot(x, wg_ref[0], preferred_element_type=F32) + bg_ref[0]
        hl = jnp.dot(x, wl_ref[0], preferred_element_type=F32) + bl_ref[0]
        xg = jnp.minimum(hg, SWIGLU_LIMIT)
        xl = jnp.clip(hl, -SWIGLU_LIMIT, SWIGLU_LIMIT)
        act = xg * jax.nn.sigmoid(SWIGLU_ALPHA * xg) * (xl + 1.0)
        ys_ref[...] = jnp.dot(act.astype(BF16), wd_ref[0].astype(BF16), preferred_element_type=F32) + bd_ref[0]

    @pl.when(i >= nv_ref[0])
    def _():
        ys_ref[...] = jnp.zeros_like(ys_ref)


def _grouped_ffn(tile_expert, n_valid, xs, wg, wl, bg, bl, wd, bd):
    n_rows, w = xs.shape
    n_exp, d, de = wg.shape
    n_tiles = n_rows // FFN_TILE
    row = lambda i, te, nv: (jnp.minimum(i, nv[0] - 1), 0)
    exp3 = lambda i, te, nv: (te[i], 0, 0)
    grid_spec = pltpu.PrefetchScalarGridSpec(
        num_scalar_prefetch=2,
        grid=(n_tiles,),
        in_specs=[
            pl.BlockSpec((FFN_TILE, w), row),
            pl.BlockSpec((1, d, de), exp3),
            pl.BlockSpec((1, d, de), exp3),
            pl.BlockSpec((1, 1, de), exp3),
            pl.BlockSpec((1, 1, de), exp3),
            pl.BlockSpec((1, de, d), exp3),
            pl.BlockSpec((1, 1, d), exp3),
        ],
        out_specs=pl.BlockSpec((FFN_TILE, d), lambda i, te, nv: (i, 0)),
    )
    return pl.pallas_call(
        _ffn_kernel,
        grid_spec=grid_spec,
        out_shape=jax.ShapeDtypeStruct((n_rows, d), F32),
        compiler_params=_params("arbitrary"),
    )(tile_expert, n_valid, xs, wg, wl, bg, bl, wd, bd)


def _combine_kernel(dest_ref, gt_ref, h1_ref, g_ref, b_ref, ys_ref, o_ref, buf_ref, sem):
    t = h1_ref.shape[0]

    def copies(i):
        for u in range(ROW_UNROLL):
            r = i * ROW_UNROLL + u
            for k in range(TOP_K):
                yield _row_copy(ys_ref, dest_ref[k, r], buf_ref.at[k], r, sem), (u * TOP_K + k) % 2

    def issue(i, carry):
        for cp, prio in copies(i):
            cp.start(priority=prio)
        return carry

    def drain(i, carry):
        for cp, _ in copies(i):
            cp.wait()
        return carry

    if t == ROW_UNROLL:
        issue(0, 0)
        drain(0, 0)
    else:
        lax.fori_loop(0, t // ROW_UNROLL, issue, 0)
        lax.fori_loop(0, t // ROW_UNROLL, drain, 0)

    gates = gt_ref[...]
    ffn = gates[:, 0:1] * buf_ref[0]
    for k in range(1, TOP_K):
        ffn = ffn + gates[:, k:k + 1] * buf_ref[k]
    o_ref[...] = _layer_norm(DEEPNORM_ALPHA * h1_ref[...] + ffn, g_ref[...], b_ref[...])


def _combine(dest, gates, h1, g, b, ys):
    n, d = h1.shape
    return pl.pallas_call(
        _combine_kernel,
        grid=(n // ROUTE_TILE,),
        in_specs=[
            pl.BlockSpec((TOP_K, ROUTE_TILE), lambda i: (0, i), memory_space=pltpu.SMEM),
            pl.BlockSpec((ROUTE_TILE, TOP_K), lambda i: (i, 0)),
            pl.BlockSpec((ROUTE_TILE, d), lambda i: (i, 0)),
            pl.BlockSpec((1, d), lambda i: (0, 0)),
            pl.BlockSpec((1, d), lambda i: (0, 0)),
            pl.BlockSpec(memory_space=pl.ANY),
        ],
        out_specs=pl.BlockSpec((ROUTE_TILE, d), lambda i: (i, 0)),
        out_shape=jax.ShapeDtypeStruct((n, d), F32),
        scratch_shapes=[pltpu.VMEM((TOP_K, ROUTE_TILE, d), F32), pltpu.SemaphoreType.DMA(())],
        compiler_params=_params("arbitrary"),
    )(dest, gates, h1, g, b, ys)


def kernel(x, emb_ln_g, emb_ln_b, w_in, attn_sink, hgrn_lb_logits, hgrn_norm_g, w_out, ln1_g, ln1_b,
           w_router, b_router, w_up, b_up, w_down, b_down, ln2_g, ln2_b):
    bsz, seq, d = x.shape
    n = bsz * seq
    n_exp = w_router.shape[-1]
    assert w_in.shape[0] == DEPTH == 1
    assert seq % (2 * ATTN_BLOCK) == 0 and seq % (HGRN_CHUNK * HGRN_GROUP) == 0
    assert n % TOKEN_TILE == 0 and n % ROUTE_TILE == 0
    row = lambda v: v.reshape(1, -1).astype(F32)

    x2 = x.reshape(n, d)
    g0, b0 = row(emb_ln_g), row(emb_ln_b)

    proj = _inproj(x2, g0, b0, w_in[0].astype(BF16))
    proj3 = proj.reshape(bsz, seq, -1)
    attn = _attention(proj3, attn_sink[0].astype(F32))

    lb = jnp.cumsum(jax.nn.softmax(hgrn_lb_logits.astype(F32), axis=1), axis=1)[:, 0]
    rec = _hgrn(proj3, lb, row(hgrn_norm_g[0]))

    wr = w_router[0].astype(F32)
    wr_hi = wr.astype(BF16)
    wr_lo = (wr - wr_hi.astype(F32)).astype(BF16)
    h1, packed, top_i, gates = _outproj(
        x2, attn.reshape(n, -1), rec.reshape(n, -1), w_out[0].astype(BF16),
        g0, b0, row(ln1_g[0]), row(ln1_b[0]), jnp.stack([wr_hi, wr_lo]), row(b_router[0]))

    rank_ti, counts = _ranks(top_i, n_exp)
    counts = counts[0]
    padded = ((counts + FFN_TILE - 1) // FFN_TILE) * FFN_TILE
    ends = jnp.cumsum(padded)
    starts = ends - padded
    n_tiles = (n * TOP_K) // FFN_TILE + n_exp
    n_rows = n_tiles * FFN_TILE
    dest = rank_ti[:TOP_K]
    for e in range(n_exp):
        dest = dest + jnp.where(rank_ti[TOP_K:] == e, starts[e], 0)
    tile_start = jnp.arange(n_tiles, dtype=jnp.int32) * FFN_TILE
    tile_expert = jnp.minimum(
        jnp.sum(tile_start[:, None] >= ends[None, :], axis=1), n_exp - 1).astype(jnp.int32)
    n_valid = (ends[-1:] // FFN_TILE).astype(jnp.int32)

    xs, wg, wl = _scatter_rows_split_weights(dest, packed, n_rows, w_up[0].astype(F32))
    bu = b_up[0].astype(F32)
    ys = _grouped_ffn(
        tile_expert, n_valid, xs, wg, wl,
        bu[:, None, 0::2], bu[:, None, 1::2],
        w_down[0].astype(F32), b_down[0].astype(F32)[:, None, :])

    out = _combine(dest, gates, h1, row(ln2_g[0]), row(ln2_b[0]), ys)
    return out.reshape(bsz, seq, d)
```

```python
import functools
import math

import jax
import jax.numpy as jnp
from jax import lax
from jax.experimental import pallas as pl
from jax.experimental.pallas import tpu as pltpu

F32 = jnp.float32
BF16 = jnp.bfloat16

ATTN_HEAD_DIM = 64
N_Q_HEADS = 8
N_KV_HEADS = 2
ATTN_WIDTH = N_Q_HEADS * ATTN_HEAD_DIM
KV_WIDTH = N_KV_HEADS * ATTN_HEAD_DIM
WINDOW = 128
ATTN_BLOCK = 128
HGRN_HEAD_DIM = 128
N_HGRN_HEADS = 4
HGRN_WIDTH = N_HGRN_HEADS * HGRN_HEAD_DIM
HGRN_CHUNK = 64
TOP_K = 4
SWIGLU_LIMIT = 7.0
SWIGLU_ALPHA = 1.702
LN_EPS = 1e-5
RMS_EPS = 1e-6
DEPTH = 1
DEEPNORM_ALPHA = (2.0 * DEPTH) ** 0.25

VMEM_LIMIT_BYTES = 48 * 1024 * 1024
TOKEN_TILE = 512
ROW_PARTS = 4
ROUTE_TILE = 256
FFN_TILE = 512
SPLIT_BLOCK = 256
ROW_UNROLL = 256
HGRN_GROUP = 32


def _params(*sem):
    return pltpu.CompilerParams(dimension_semantics=sem, vmem_limit_bytes=VMEM_LIMIT_BYTES)


def _layer_norm(x, g, b):
    mu = jnp.mean(x, axis=-1, keepdims=True)
    xc = x - mu
    var = jnp.mean(xc * xc, axis=-1, keepdims=True)
    return xc * lax.rsqrt(var + LN_EPS) * g + b


def _split3(x):
    hi = x.astype(BF16)
    r = x - hi.astype(F32)
    mid = r.astype(BF16)
    lo = (r - mid.astype(F32)).astype(BF16)
    return hi, mid, lo


def _pack_bf16_pairs(x):
    d2 = x.shape[1] // 2
    lo_bits = lax.bitcast_convert_type(x[:, :d2].astype(BF16).astype(F32), jnp.uint32)
    hi_bits = lax.bitcast_convert_type(x[:, d2:].astype(BF16).astype(F32), jnp.uint32)
    return hi_bits | (lo_bits >> 16)


def _unpack_bf16_pairs(u):
    lo = lax.bitcast_convert_type(u << 16, F32)
    hi = lax.bitcast_convert_type(u & jnp.uint32(0xFFFF0000), F32)
    return lo, hi


def _inproj_kernel(x_ref, g_ref, b_ref, w_ref, o_ref, *, col_chunk):
    n_rows, n_cols = o_ref.shape
    part = n_rows // ROW_PARTS
    for r in range(0, n_rows, part):
        h = _layer_norm(x_ref[r:r + part, :], g_ref[...], b_ref[...]).astype(BF16)
        for c in range(0, n_cols, col_chunk):
            o_ref[r:r + part, c:c + col_chunk] = jnp.dot(
                h, w_ref[:, c:c + col_chunk], preferred_element_type=F32).astype(o_ref.dtype)


def _inproj(x2, g, b, w_bf16):
    n, d = x2.shape
    cols = w_bf16.shape[1]
    return pl.pallas_call(
        functools.partial(_inproj_kernel, col_chunk=256),
        grid=(n // TOKEN_TILE,),
        in_specs=[
            pl.BlockSpec((TOKEN_TILE, d), lambda i: (i, 0)),
            pl.BlockSpec((1, d), lambda i: (0, 0)),
            pl.BlockSpec((1, d), lambda i: (0, 0)),
            pl.BlockSpec((d, cols), lambda i: (0, 0)),
        ],
        out_specs=pl.BlockSpec((TOKEN_TILE, cols), lambda i: (i, 0)),
        out_shape=jax.ShapeDtypeStruct((n, cols), BF16),
        compiler_params=_params("parallel"),
    )(x2, g, b, w_bf16)


def _attn_block(q, kv, bias_ref, sink_ref):
    blk, hd = ATTN_BLOCK, ATTN_HEAD_DIM
    rep = N_Q_HEADS // N_KV_HEADS
    q = q * (1.0 / math.sqrt(hd))
    ones = jnp.ones((8, 3 * blk), BF16)
    outs = []
    for g in range(N_KV_HEADS):
        heads = range(g * rep, (g + 1) * rep)
        qg = jnp.concatenate([q[:, h * hd:(h + 1) * hd] for h in heads], axis=0)
        kg = kv[:, g * hd:(g + 1) * hd]
        vg = kv[:, KV_WIDTH + g * hd:KV_WIDTH + (g + 1) * hd]
        st = lax.dot_general(kg, qg, (((1,), (1,)), ((), ())), preferred_element_type=F32) + bias_ref[0, g]
        sink = jnp.concatenate([jnp.full((1, blk), sink_ref[h], F32) for h in heads], axis=1)
        m = jnp.maximum(jnp.max(st, axis=0, keepdims=True), sink)
        e = jnp.exp(st - m).astype(BF16)
        pv = lax.dot_general(vg, e, (((0,), (0,)), ((), ())), preferred_element_type=F32)
        den = jnp.dot(ones, e, preferred_element_type=F32)[0:1] + jnp.exp(sink - m)
        og = pv / den
        outs.extend(og[:, r * blk:(r + 1) * blk].T for r in range(rep))
    return jnp.concatenate(outs, axis=1)


def _attn_kernel(sink_ref, bias_a_ref, bias_b_ref, q_ref, k0_ref, k1_ref, k2_ref, k3_ref, o_ref):
    blk = ATTN_BLOCK
    k0, k1, k2, k3 = k0_ref[0], k1_ref[0], k2_ref[0], k3_ref[0]
    o_a = _attn_block(q_ref[0, :blk, :], jnp.concatenate([k0, k1, k2], axis=0), bias_a_ref, sink_ref)
    o_b = _attn_block(q_ref[0, blk:, :], jnp.concatenate([k1, k2, k3], axis=0), bias_b_ref, sink_ref)
    o_ref[0, :blk, :] = o_a.astype(o_ref.dtype)
    o_ref[0, blk:, :] = o_b.astype(o_ref.dtype)


def _alibi_window_bias():
    blk = ATTN_BLOCK
    rep = N_Q_HEADS // N_KV_HEADS
    krel = jnp.arange(3 * blk)[:, None] - blk
    qpos = jnp.arange(blk)[None, :]
    dist = jnp.abs(krel - qpos)
    slopes = jnp.asarray([2.0 ** (-8.0 * (h + 1) / N_Q_HEADS) for h in range(N_Q_HEADS)], F32)
    bias = jnp.where(dist <= WINDOW, -slopes[:, None, None] * dist.astype(F32), -jnp.inf)
    bias = bias.reshape(N_KV_HEADS, rep, 3 * blk, blk).transpose(0, 2, 1, 3).reshape(N_KV_HEADS, 3 * blk, rep * blk)
    no_prev = jnp.where(krel >= 0, 0.0, -jnp.inf)
    no_next = jnp.where(krel < blk, 0.0, -jnp.inf)
    return jnp.stack([bias, bias + no_prev, bias + no_next])


def _attention(proj3, sink):
    b, s, _ = proj3.shape
    nb = s // ATTN_BLOCK
    nb2 = nb // 2
    kv_col = ATTN_WIDTH // (2 * KV_WIDTH)
    kv_spec = lambda off: pl.BlockSpec(
        (1, ATTN_BLOCK, 2 * KV_WIDTH), lambda i, j: (i, jnp.clip(2 * j + off, 0, nb - 1), kv_col))
    bias = _alibi_window_bias()
    bias_spec = lambda f: pl.BlockSpec((1,) + bias.shape[1:], f)
    return pl.pallas_call(
        _attn_kernel,
        grid=(b, nb2),
        in_specs=[
            pl.BlockSpec(memory_space=pltpu.SMEM),
            bias_spec(lambda i, j: (jnp.where(j == 0, 1, 0), 0, 0, 0)),
            bias_spec(lambda i, j: (jnp.where(j == nb2 - 1, 2, 0), 0, 0, 0)),
            pl.BlockSpec((1, 2 * ATTN_BLOCK, ATTN_WIDTH), lambda i, j: (i, j, 0)),
            kv_spec(-1), kv_spec(0), kv_spec(1), kv_spec(2),
        ],
        out_specs=pl.BlockSpec((1, 2 * ATTN_BLOCK, ATTN_WIDTH), lambda i, j: (i, j, 0)),
        out_shape=jax.ShapeDtypeStruct((b, s, ATTN_WIDTH), BF16),
        compiler_params=_params("parallel", "parallel"),
    )(sink, bias, bias, proj3, proj3, proj3, proj3, proj3)


def _hgrn_group(q, v, z, lb, state_t, tri, forward):
    g, c, dk = q.shape
    f = lb + (1.0 - lb) * jax.nn.sigmoid(z)
    log_f = jnp.log(f)
    k = 1.0 - f
    tri_g = jnp.broadcast_to(tri[None], (g, c, c))
    pieces = jnp.concatenate(_split3(log_f), axis=2)
    cum3 = jnp.einsum("gts,gsd->gtd", tri_g, pieces, preferred_element_type=F32)
    cum = cum3[:, :, :dk] + cum3[:, :, dk:2 * dk] + cum3[:, :, 2 * dk:]
    if forward:
        ref = cum[:, c // 2:c // 2 + 1, :]
        last = cum[:, c - 1:c, :]
    else:
        ref = cum[:, c - 1 - c // 2:c - c // 2, :]
        last = cum[:, 0:1, :]
    q_rel = (q * jnp.exp(cum - ref)).astype(BF16)
    k_rel = (k * jnp.exp(ref - cum)).astype(BF16)
    a = jnp.einsum("gtd,gsd->gts", q_rel, k_rel, preferred_element_type=F32)
    a = jnp.where(tri_g > 0, a, 0.0).astype(BF16)
    o_intra = jnp.einsum("gts,gsv->gtv", a, v, preferred_element_type=F32)
    k_dec = (k * jnp.exp(last - cum)).astype(BF16)
    kv_t = jnp.einsum("gsv,gsd->gvd", v, k_dec, preferred_element_type=F32)
    q_dec = (q * jnp.exp(cum)).astype(BF16)
    decay = jnp.exp(last)
    o_inter = [None] * g
    for j in (range(g) if forward else range(g - 1, -1, -1)):
        o_inter[j] = lax.dot_general(q_dec[j], state_t.astype(BF16), (((1,), (1,)), ((), ())),
                                     preferred_element_type=F32)
        state_t = state_t * decay[j] + kv_t[j]
    return o_intra + jnp.stack(o_inter, axis=0), state_t


def _hgrn_kernel(q_ref, v_ref, zf_ref, zb_ref, zg_ref, lb_ref, ng_ref, o_ref, of_ref, ob_ref):
    c = HGRN_CHUNK
    g = HGRN_GROUP
    seq, dk = q_ref.shape[1], q_ref.shape[2]
    dv = v_ref.shape[2]
    n_groups = seq // (g * c)
    row = lax.broadcasted_iota(jnp.int32, (c, c), 0)
    col = lax.broadcasted_iota(jnp.int32, (c, c), 1)
    lower = jnp.where(row >= col, 1.0, 0.0).astype(BF16)
    upper = jnp.where(row <= col, 1.0, 0.0).astype(BF16)
    lb_f = lb_ref[0:1, :]
    lb_b = lb_ref[1:2, :]

    def group(ref, sl, width):
        return ref[0, sl, :].astype(F32).reshape(g, c, width)

    def body(i, carry):
        st_f, st_b = carry
        sl_f = pl.ds(pl.multiple_of(i * (g * c), g * c), g * c)
        sl_b = pl.ds(pl.multiple_of((n_groups - 1 - i) * (g * c), g * c), g * c)
        o_f, st_f = _hgrn_group(group(q_ref, sl_f, dk), v_ref[0, sl_f, :].reshape(g, c, dv),
                                group(zf_ref, sl_f, dk), lb_f, st_f, lower, True)
        o_b, st_b = _hgrn_group(group(q_ref, sl_b, dk), v_ref[0, sl_b, :].reshape(g, c, dv),
                                group(zb_ref, sl_b, dk), lb_b, st_b, upper, False)
        of_ref[sl_f, :] = o_f.reshape(g * c, dv)
        ob_ref[sl_b, :] = o_b.reshape(g * c, dv)
        return st_f, st_b

    zero = jnp.zeros((dv, dk), F32)
    lax.fori_loop(0, n_groups, body, (zero, zero))

    o = of_ref[...] + ob_ref[...]
    o = o * lax.rsqrt(jnp.mean(o * o, axis=-1, keepdims=True) + RMS_EPS)
    zg = zg_ref[0].astype(F32)
    o_ref[0] = (o * ng_ref[...] * (zg * jax.nn.sigmoid(zg))).astype(o_ref.dtype)


def _hgrn(proj3, lb, norm_g):
    b, s, _ = proj3.shape
    hd = HGRN_HEAD_DIM
    base = (ATTN_WIDTH + 2 * KV_WIDTH) // hd
    col_spec = lambda part: pl.BlockSpec((1, s, hd), lambda i, h: (i, 0, base + part * N_HGRN_HEADS + h))
    return pl.pallas_call(
        _hgrn_kernel,
        grid=(b, N_HGRN_HEADS),
        in_specs=[
            col_spec(0), col_spec(1), col_spec(2), col_spec(3), col_spec(4),
            pl.BlockSpec((2, hd), lambda i, h: (0, h)),
            pl.BlockSpec((1, hd), lambda i, h: (0, h)),
        ],
        out_specs=pl.BlockSpec((1, s, hd), lambda i, h: (i, 0, h)),
        out_shape=jax.ShapeDtypeStruct((b, s, HGRN_WIDTH), BF16),
        scratch_shapes=[pltpu.VMEM((s, hd), F32), pltpu.VMEM((s, hd), F32)],
        compiler_params=_params("parallel", "parallel"),
    )(proj3, proj3, proj3, proj3, proj3, lb, norm_g)


def _outproj_kernel(x_ref, attn_ref, rec_ref, wo_ref, g0_ref, b0_ref, g1_ref, b1_ref,
                    wr_ref, br_ref, h1_ref, pk_ref, ti_ref, gt_ref):
    half = attn_ref.shape[1]
    h0 = _layer_norm(x_ref[...], g0_ref[...], b0_ref[...])
    mix = (jnp.dot(attn_ref[...], wo_ref[:half, :], preferred_element_type=F32)
           + jnp.dot(rec_ref[...], wo_ref[half:, :], preferred_element_type=F32))
    h1 = _layer_norm(DEEPNORM_ALPHA * h0 + mix, g1_ref[...], b1_ref[...])
    h1_ref[...] = h1

    pk_ref[...] = _pack_bf16_pairs(h1)

    h_hi = h1.astype(BF16)
    h_lo = (h1 - h_hi.astype(F32)).astype(BF16)
    w_hi = wr_ref[0]
    w_lo = wr_ref[1]
    nt = (((0,), (1,)), ((), ()))
    logits = (lax.dot_general(w_hi, h_hi, nt, preferred_element_type=F32)
              + lax.dot_general(w_lo, h_hi, nt, preferred_element_type=F32)
              + lax.dot_general(w_hi, h_lo, nt, preferred_element_type=F32)) + br_ref[...].T

    n_exp = logits.shape[0]
    sub = lax.broadcasted_iota(jnp.int32, logits.shape, 0)
    vals, idxs = [], []
    cur = logits
    for _ in range(TOP_K):
        m = jnp.max(cur, axis=0, keepdims=True)
        idx = jnp.min(jnp.where(cur == m, sub, n_exp), axis=0, keepdims=True)
        vals.append(m)
        idxs.append(idx)
        cur = jnp.where(sub == idx, -jnp.inf, cur)
    top_v = jnp.concatenate(vals, axis=0)
    e = jnp.exp(top_v - vals[0])
    gates_t = e / jnp.sum(e, axis=0, keepdims=True)
    both = jnp.concatenate([gates_t, jnp.concatenate(idxs, axis=0).astype(F32)], axis=0)
    both = both.T
    gt_ref[...] = both[:, :TOP_K]
    ti_ref[...] = both[:, TOP_K:].astype(jnp.int32)


def _outproj(x2, attn2, rec2, wo_bf16, g0, b0, g1, b1, wr2, br):
    n, d = x2.shape
    half = attn2.shape[1]
    n_exp = br.shape[1]
    tile = lambda w: pl.BlockSpec((TOKEN_TILE, w), lambda i: (i, 0))
    vec = pl.BlockSpec((1, d), lambda i: (0, 0))
    return pl.pallas_call(
        _outproj_kernel,
        grid=(n // TOKEN_TILE,),
        in_specs=[
            tile(d), tile(half), tile(half),
            pl.BlockSpec((d, d), lambda i: (0, 0)),
            vec, vec, vec, vec,
            pl.BlockSpec((2, d, n_exp), lambda i: (0, 0, 0)),
            pl.BlockSpec((1, n_exp), lambda i: (0, 0)),
        ],
        out_specs=[tile(d), tile(d // 2), tile(TOP_K), tile(TOP_K)],
        out_shape=[
            jax.ShapeDtypeStruct((n, d), F32),
            jax.ShapeDtypeStruct((n, d // 2), jnp.uint32),
            jax.ShapeDtypeStruct((n, TOP_K), jnp.int32),
            jax.ShapeDtypeStruct((n, TOP_K), F32),
        ],
        compiler_params=_params("parallel"),
    )(x2, attn2, rec2, wo_bf16, g0, b0, g1, b1, wr2, br)


def _rank_kernel(ti_ref, rank_ref, cnt_ref, run_ref, *, n_exp):
    i = pl.program_id(0)

    @pl.when(i == 0)
    def _():
        run_ref[...] = jnp.zeros_like(run_ref)

    ti = ti_ref[...]
    t = ti.shape[0]
    lane = lax.broadcasted_iota(jnp.int32, (t, n_exp), 1)
    hots = [jnp.where(lane == ti[:, k:k + 1], 1.0, 0.0) for k in range(TOP_K)]
    member = hots[0]
    for k in range(1, TOP_K):
        member = member + hots[k]
    row = lax.broadcasted_iota(jnp.int32, (t, t), 0)
    col = lax.broadcasted_iota(jnp.int32, (t, t), 1)
    strict_lower = jnp.where(row > col, 1.0, 0.0).astype(BF16)
    before = jnp.dot(strict_lower, member.astype(BF16), preferred_element_type=F32) + run_ref[...]
    ranks = [jnp.sum(hots[k] * before, axis=-1, keepdims=True) for k in range(TOP_K)]
    cols = jnp.concatenate(ranks + [ti.astype(F32), jnp.zeros((t, 128 - 2 * TOP_K), F32)], axis=1)
    rank_ref[...] = cols.T[:2 * TOP_K, :].astype(jnp.int32)
    run_ref[...] = run_ref[...] + jnp.sum(member, axis=0, keepdims=True)
    cnt_ref[...] = run_ref[...].astype(jnp.int32)


def _ranks(top_i, n_exp):
    n = top_i.shape[0]
    return pl.pallas_call(
        functools.partial(_rank_kernel, n_exp=n_exp),
        grid=(n // TOKEN_TILE,),
        in_specs=[pl.BlockSpec((TOKEN_TILE, TOP_K), lambda i: (i, 0))],
        out_specs=[pl.BlockSpec((2 * TOP_K, TOKEN_TILE), lambda i: (0, i)),
                   pl.BlockSpec((1, n_exp), lambda i: (0, 0))],
        out_shape=[jax.ShapeDtypeStruct((2 * TOP_K, n), jnp.int32),
                   jax.ShapeDtypeStruct((1, n_exp), jnp.int32)],
        scratch_shapes=[pltpu.VMEM((1, n_exp), F32)],
        compiler_params=_params("arbitrary"),
    )(top_i)


def _row_copy(src_ref, src_row, dst_ref, dst_row, sem):
    return pltpu.make_async_copy(src_ref.at[pl.ds(src_row, 1), :], dst_ref.at[pl.ds(dst_row, 1), :], sem)


def _scatter_split_kernel(dest_ref, x_ref, w_ref, init_ref, xs_ref, wg_ref, wl_ref, sem, *, blk):
    del init_ref
    t = x_ref.shape[0]

    def copies(i):
        for u in range(ROW_UNROLL):
            r = i * ROW_UNROLL + u
            for k in range(TOP_K):
                yield _row_copy(x_ref, r, xs_ref, dest_ref[k, r], sem), (u * TOP_K + k) % 2

    def issue(i, carry):
        for cp, prio in copies(i):
            cp.start(priority=prio)
        return carry

    def drain(i, carry):
        for cp, _ in copies(i):
            cp.wait()
        return carry

    if t == ROW_UNROLL:
        issue(0, 0)
    else:
        lax.fori_loop(0, t // ROW_UNROLL, issue, 0)
    _split_up_columns(w_ref, wg_ref, wl_ref, blk)
    if t == ROW_UNROLL:
        drain(0, 0)
    else:
        lax.fori_loop(0, t // ROW_UNROLL, drain, 0)


def _split_up_columns(w_ref, wg_ref, wl_ref, blk):
    r = lax.broadcasted_iota(jnp.int32, (2 * blk, blk), 0)
    c = lax.broadcasted_iota(jnp.int32, (2 * blk, blk), 1)
    pick_even = jnp.where(r == 2 * c, 1.0, 0.0).astype(BF16)
    pick_odd = jnp.where(r == 2 * c + 1, 1.0, 0.0).astype(BF16)
    for j in range(wg_ref.shape[2] // blk):
        w = w_ref[0, :, 2 * blk * j:2 * blk * (j + 1)].astype(BF16)
        wg_ref[0, :, blk * j:blk * (j + 1)] = jnp.dot(w, pick_even, preferred_element_type=F32).astype(BF16)
        wl_ref[0, :, blk * j:blk * (j + 1)] = jnp.dot(w, pick_odd, preferred_element_type=F32).astype(BF16)


def _scatter_rows_split_weights(dest, packed, n_rows, w_up):
    n, w = packed.shape
    n_exp, d, de2 = w_up.shape
    de = de2 // 2
    steps = n // ROUTE_TILE
    slabs = steps // n_exp
    assert steps == slabs * n_exp and de % (slabs * SPLIT_BLOCK) == 0
    init = jnp.zeros((n_rows, w), packed.dtype)
    w_out = jax.ShapeDtypeStruct((n_exp, d, de), BF16)
    slab = lambda width: pl.BlockSpec((1, d, width), lambda i: (i // slabs, 0, i % slabs))
    return pl.pallas_call(
        functools.partial(_scatter_split_kernel, blk=SPLIT_BLOCK),
        grid=(steps,),
        in_specs=[
            pl.BlockSpec((TOP_K, ROUTE_TILE), lambda i: (0, i), memory_space=pltpu.SMEM),
            pl.BlockSpec((ROUTE_TILE, w), lambda i: (i, 0)),
            slab(de2 // slabs),
            pl.BlockSpec(memory_space=pl.ANY),
        ],
        out_specs=[pl.BlockSpec(memory_space=pl.ANY), slab(de // slabs), slab(de // slabs)],
        out_shape=[jax.ShapeDtypeStruct((n_rows, w), packed.dtype), w_out, w_out],
        scratch_shapes=[pltpu.SemaphoreType.DMA(())],
        input_output_aliases={3: 0},
        compiler_params=_params("arbitrary"),
    )(dest, packed, w_up, init)


def _ffn_kernel(te_ref, nv_ref, xs_ref, wg_ref, wl_ref, bg_ref, bl_ref, wd_ref, bd_ref, ys_ref):
    i = pl.program_id(0)

    @pl.when(i < nv_ref[0])
    def _():
        lo, hi = _unpack_bf16_pairs(xs_ref[...])
        x = jnp.concatenate([lo.astype(BF16), hi.astype(BF16)], axis=1)
        hg = jnp.dot(x, wg_ref[0], preferred_element_type=F32) + bg_ref[0]
        hl = jnp.dot(x, wl_ref[0], preferred_element_type=F32) + bl_ref[0]
        xg = jnp.minimum(hg, SWIGLU_LIMIT)
        xl = jnp.clip(hl, -SWIGLU_LIMIT, SWIGLU_LIMIT)
        act = xg * jax.nn.sigmoid(SWIGLU_ALPHA * xg) * (xl + 1.0)
        ys_ref[...] = jnp.dot(act.astype(BF16), wd_ref[0].astype(BF16), preferred_element_type=F32) + bd_ref[0]

    @pl.when(i >= nv_ref[0])
    def _():
        ys_ref[...] = jnp.zeros_like(ys_ref)


def _grouped_ffn(tile_expert, n_valid, xs, wg, wl, bg, bl, wd, bd):
    n_rows, w = xs.shape
    n_exp, d, de = wg.shape
    n_tiles = n_rows // FFN_TILE
    row = lambda i, te, nv: (jnp.minimum(i, nv[0] - 1), 0)
    exp3 = lambda i, te, nv: (te[i], 0, 0)
    grid_spec = pltpu.PrefetchScalarGridSpec(
        num_scalar_prefetch=2,
        grid=(n_tiles,),
        in_specs=[
            pl.BlockSpec((FFN_TILE, w), row),
            pl.BlockSpec((1, d, de), exp3),
            pl.BlockSpec((1, d, de), exp3),
            pl.BlockSpec((1, 1, de), exp3),
            pl.BlockSpec((1, 1, de), exp3),
            pl.BlockSpec((1, de, d), exp3),
            pl.BlockSpec((1, 1, d), exp3),
        ],
        out_specs=pl.BlockSpec((FFN_TILE, d), lambda i, te, nv: (i, 0)),
    )
    return pl.pallas_call(
        _ffn_kernel,
        grid_spec=grid_spec,
        out_shape=jax.ShapeDtypeStruct((n_rows, d), F32),
        compiler_params=_params("arbitrary"),
    )(tile_expert, n_valid, xs, wg, wl, bg, bl, wd, bd)


def _combine_kernel(dest_ref, gt_ref, h1_ref, g_ref, b_ref, ys_ref, o_ref, buf_ref, sem):
    t = h1_ref.shape[0]

    def copies(i):
        for u in range(ROW_UNROLL):
            r = i * ROW_UNROLL + u
            for k in range(TOP_K):
                yield _row_copy(ys_ref, dest_ref[k, r], buf_ref.at[k], r, sem), (u * TOP_K + k) % 2

    def issue(i, carry):
        for cp, prio in copies(i):
            cp.start(priority=prio)
        return carry

    def drain(i, carry):
        for cp, _ in copies(i):
            cp.wait()
        return carry

    if t == ROW_UNROLL:
        issue(0, 0)
        drain(0, 0)
    else:
        lax.fori_loop(0, t // ROW_UNROLL, issue, 0)
        lax.fori_loop(0, t // ROW_UNROLL, drain, 0)

    gates = gt_ref[...]
    ffn = gates[:, 0:1] * buf_ref[0]
    for k in range(1, TOP_K):
        ffn = ffn + gates[:, k:k + 1] * buf_ref[k]
    o_ref[...] = _layer_norm(DEEPNORM_ALPHA * h1_ref[...] + ffn, g_ref[...], b_ref[...])


def _combine(dest, gates, h1, g, b, ys):
    n, d = h1.shape
    return pl.pallas_call(
        _combine_kernel,
        grid=(n // ROUTE_TILE,),
        in_specs=[
            pl.BlockSpec((TOP_K, ROUTE_TILE), lambda i: (0, i), memory_space=pltpu.SMEM),
            pl.BlockSpec((ROUTE_TILE, TOP_K), lambda i: (i, 0)),
            pl.BlockSpec((ROUTE_TILE, d), lambda i: (i, 0)),
            pl.BlockSpec((1, d), lambda i: (0, 0)),
            pl.BlockSpec((1, d), lambda i: (0, 0)),
            pl.BlockSpec(memory_space=pl.ANY),
        ],
        out_specs=pl.BlockSpec((ROUTE_TILE, d), lambda i: (i, 0)),
        out_shape=jax.ShapeDtypeStruct((n, d), F32),
        scratch_shapes=[pltpu.VMEM((TOP_K, ROUTE_TILE, d), F32), pltpu.SemaphoreType.DMA(())],
        compiler_params=_params("arbitrary"),
    )(dest, gates, h1, g, b, ys)


def kernel(x, emb_ln_g, emb_ln_b, w_in, attn_sink, hgrn_lb_logits, hgrn_norm_g, w_out, ln1_g, ln1_b,
           w_router, b_router, w_up, b_up, w_down, b_down, ln2_g, ln2_b):
    bsz, seq, d = x.shape
    n = bsz * seq
    n_exp = w_router.shape[-1]
    assert w_in.shape[0] == DEPTH == 1
    assert seq % (2 * ATTN_BLOCK) == 0 and seq % (HGRN_CHUNK * HGRN_GROUP) == 0
    assert n % TOKEN_TILE == 0 and n % ROUTE_TILE == 0
    row = lambda v: v.reshape(1, -1).astype(F32)

    x2 = x.reshape(n, d)
    g0, b0 = row(emb_ln_g), row(emb_ln_b)

    proj = _inproj(x2, g0, b0, w_in[0].astype(BF16))
    proj3 = proj.reshape(bsz, seq, -1)
    attn = _attention(proj3, attn_sink[0].astype(F32))

    lb = jnp.cumsum(jax.nn.softmax(hgrn_lb_logits.astype(F32), axis=1), axis=1)[:, 0]
    rec = _hgrn(proj3, lb, row(hgrn_norm_g[0]))

    wr = w_router[0].astype(F32)
    wr_hi = wr.astype(BF16)
    wr_lo = (wr - wr_hi.astype(F32)).astype(BF16)
    h1, packed, top_i, gates = _outproj(
        x2, attn.reshape(n, -1), rec.reshape(n, -1), w_out[0].astype(BF16),
        g0, b0, row(ln1_g[0]), row(ln1_b[0]), jnp.stack([wr_hi, wr_lo]), row(b_router[0]))

    rank_ti, counts = _ranks(top_i, n_exp)
    counts = counts[0]
    padded = ((counts + FFN_TILE - 1) // FFN_TILE) * FFN_TILE
    ends = jnp.cumsum(padded)
    starts = ends - padded
    n_tiles = (n * TOP_K) // FFN_TILE + n_exp
    n_rows = n_tiles * FFN_TILE
    dest = rank_ti[:TOP_K]
    for e in range(n_exp):
        dest = dest + jnp.where(rank_ti[TOP_K:] == e, starts[e], 0)
    tile_start = jnp.arange(n_tiles, dtype=jnp.int32) * FFN_TILE
    tile_expert = jnp.minimum(
        jnp.sum(tile_start[:, None] >= ends[None, :], axis=1), n_exp - 1).astype(jnp.int32)
    n_valid = (ends[-1:] // FFN_TILE).astype(jnp.int32)

    xs, wg, wl = _scatter_rows_split_weights(dest, packed, n_rows, w_up[0].astype(F32))
    bu = b_up[0].astype(F32)
    ys = _grouped_ffn(
        tile_expert, n_valid, xs, wg, wl,
        bu[:, None, 0::2], bu[:, None, 1::2],
        w_down[0].astype(F32), b_down[0].astype(F32)[:, None, :])

    out = _combine(dest, gates, h1, row(ln2_g[0]), row(ln2_b[0]), ys)
    return out.reshape(bsz, seq, d)
```
